```python
import jax
import jax.numpy as jnp
from jax import lax
import numpy as np

D_MODEL = 2048
BATCH = 2
SEQ = 8192
DEPTH = 1
DEC_BATCH = 128
DEC_SEQ = 4
PAST_LEN = 16384
PAGE_SIZE = 128

HEAD_DIM = 64
N_HEADS_TOTAL = D_MODEL // HEAD_DIM
N_HEADS_A = N_HEADS_TOTAL // 2
N_KV_A = max(1, N_HEADS_A // 8)
GROUP_A = N_HEADS_A // N_KV_A
N_HEADS_B = N_HEADS_TOTAL - N_HEADS_A
WIDTH_A = N_HEADS_A * HEAD_DIM
WIDTH_B = N_HEADS_B * HEAD_DIM
KV_WIDTH_A = N_KV_A * HEAD_DIM
PROJ_SIZES = (WIDTH_A, KV_WIDTH_A, KV_WIDTH_A, WIDTH_B, WIDTH_B, WIDTH_B)
PROJ_WIDTH = sum(PROJ_SIZES)
MIX_WIDTH = WIDTH_A + WIDTH_B
WINDOW_A = 128
DILATED_PAIRS = ((128, 1), (512, 4), (2048, 16))
WINDOW_B = max(w for w, _ in DILATED_PAIRS)
BLOCK = 128
ROPE_THETA = 10000.0
N_GROUPS = 4
EXPERTS_PER_GROUP = 8
N_EXPERTS = N_GROUPS * EXPERTS_PER_GROUP
TOP_K = 2
D_EXPERT = D_MODEL // 2
MOE_BLOCK = 128
RMS_EPS = 1e-6
NEG_INF = -1e30

kernel_name = 'hymba_swa_sink_dilated_hmoe_step'


def _rms_norm(x, g):
    xf = x.astype(jnp.float32)
    y = xf * lax.rsqrt(jnp.mean(xf * xf, axis=-1, keepdims=True) + RMS_EPS)
    return (y * g.astype(jnp.float32)).astype(x.dtype)


def _rope(x, pos):
    half = HEAD_DIM // 2
    inv_freq = ROPE_THETA ** (-jnp.arange(half, dtype=jnp.float32) / half)
    ang = pos.astype(jnp.float32)[:, None] * inv_freq[None, :]
    cos = jnp.cos(ang)[None, :, None, :]
    sin = jnp.sin(ang)[None, :, None, :]
    xf = x.astype(jnp.float32)
    x1, x2 = xf[..., :half], xf[..., half:]
    return jnp.concatenate([x1 * cos - x2 * sin, x2 * cos + x1 * sin], axis=-1).astype(x.dtype)


def _masked_softmax_parts(s, mask):
    s = jnp.where(mask, s, NEG_INF)
    m = jnp.max(s, axis=-1, keepdims=True)
    p = jnp.exp(s - m)
    den = jnp.sum(p, axis=-1)
    return p, m[..., 0] + jnp.log(den), den


def _banded_window_attention(q, k, v, window):
    n, l, kv, g, dh = q.shape
    nb = l // BLOCK
    qb = q.reshape(n, nb, BLOCK, kv, g, dh)

    def two_blocks(x):
        xp = jnp.pad(x, ((0, 0), (BLOCK, 0), (0, 0), (0, 0))).reshape(n, nb + 1, BLOCK, kv, dh)
        return jnp.concatenate([xp[:, :-1], xp[:, 1:]], axis=2)

    kb, vb = two_blocks(k), two_blocks(v)
    s = jnp.einsum('nbqkgd,nbskd->nbkgqs', qb, kb, preferred_element_type=jnp.float32) * (dh ** -0.5)
    qi = jnp.arange(BLOCK)[:, None]
    kj = jnp.arange(2 * BLOCK)[None, :]
    dist = qi + BLOCK - kj
    key_pos = jnp.arange(nb)[:, None, None] * BLOCK - BLOCK + kj[None]
    mask = (dist >= 0) & (dist <= window) & (key_pos >= 0)
    p, lse, den = _masked_softmax_parts(s, mask[None, :, None, None])
    o = jnp.einsum('nbkgqs,nbskd->nbqkgd', p, vb.astype(jnp.float32)) / jnp.moveaxis(den, -1, 2)[..., None]
    lse = jnp.moveaxis(lse, -1, 2)
    return o.reshape(n, l, kv, g, dh), lse.reshape(n, l, kv, g)


def _combine_by_denominator(outs, lses):
    wts = jax.nn.softmax(jnp.stack(lses, axis=0), axis=0)
    return jnp.sum(wts[..., None] * jnp.stack(outs, axis=0), axis=0)


def _window_sink_prompt(q, k, v, sinks):
    b, t = q.shape[:2]
    tp = -(-t // BLOCK) * BLOCK
    pad = lambda x: jnp.pad(x, ((0, 0), (0, tp - t), (0, 0), (0, 0)))
    qg = pad(q).reshape(b, tp, N_KV_A, GROUP_A, HEAD_DIM)
    o, lse = _banded_window_attention(qg, pad(k), pad(v), WINDOW_A)
    sink = sinks.astype(jnp.float32).reshape(N_KV_A, GROUP_A)
    o = o * jax.nn.sigmoid(lse - sink)[..., None]
    return o[:, :t].reshape(b, t, WIDTH_A).astype(q.dtype)


def _window_sink_sample(q, kc, vc, sinks):
    bd, s_len = q.shape[:2]
    qpos = PAST_LEN + jnp.arange(s_len)
    kpos = PAST_LEN - WINDOW_A + jnp.arange(kc.shape[1])
    dist = qpos[:, None] - kpos[None, :]
    mask = (dist >= 0) & (dist <= WINDOW_A) & (kpos[None, :] >= 0)
    qg = q.reshape(bd, s_len, N_KV_A, GROUP_A, HEAD_DIM)
    s = jnp.einsum('bskgd,btkd->bkgst', qg, kc, preferred_element_type=jnp.float32) * (HEAD_DIM ** -0.5)
    p, lse, den = _masked_softmax_parts(s, mask)
    o = jnp.einsum('bkgst,btkd->bskgd', p, vc.astype(jnp.float32)) / jnp.moveaxis(den, -1, 1)[..., None]
    lse = jnp.moveaxis(lse, -1, 1)
    sink = sinks.astype(jnp.float32).reshape(N_KV_A, GROUP_A)
    o = o * jax.nn.sigmoid(lse - sink)[..., None]
    return o.reshape(bd, s_len, WIDTH_A).astype(q.dtype)


def _dilated_prompt(q, k, v):
    b, t, h, dh = q.shape
    outs, lses = [], []
    for w, r in DILATED_PAIRS:
        span = r * BLOCK
        tp = -(-t // span) * span
        ls = tp // r

        def to_sub(x):
            x = jnp.pad(x, ((0, 0), (0, tp - t), (0, 0), (0, 0)))
            return x.reshape(b, ls, r, h, dh).transpose(0, 2, 1, 3, 4).reshape(b * r, ls, h, dh)

        o, lse = _banded_window_attention(to_sub(q)[:, :, :, None, :], to_sub(k), to_sub(v), w // r)
        outs.append(o.reshape(b, r, ls, h, dh).transpose(0, 2, 1, 3, 4).reshape(b, tp, h, dh)[:, :t])
        lses.append(lse.reshape(b, r, ls, h).transpose(0, 2, 1, 3).reshape(b, tp, h)[:, :t])
    o = _combine_by_denominator(outs, lses)
    return o.reshape(b, t, h * dh).astype(q.dtype)


def _dilated_sample(q, kc, vc):
    bd, s_len, h, dh = q.shape
    sidx = jnp.arange(s_len)
    outs, lses = [], []
    for w, r in DILATED_PAIRS:
        j = jnp.arange(w // r + 1)
        idx = WINDOW_B + sidx[:, None] - r * j[None, :]
        valid = (PAST_LEN - WINDOW_B + idx) >= 0
        kg = jnp.take(kc, idx, axis=1)
        vg = jnp.take(vc, idx, axis=1)
        s = jnp.einsum('bshd,bsjhd->bhsj', q, kg, preferred_element_type=jnp.float32) * (dh ** -0.5)
        p, lse, den = _masked_softmax_parts(s, valid)
        o = jnp.einsum('bhsj,bsjhd->bshd', p, vg.astype(jnp.float32)) / jnp.moveaxis(den, 1, -1)[..., None]
        outs.append(o)
        lses.append(jnp.moveaxis(lse, 1, -1))
    o = _combine_by_denominator(outs, lses)
    return o.reshape(bd, s_len, h * dh).astype(q.dtype)


def _project(h, w_in, pos):
    b, t, _ = h.shape
    p = jnp.einsum('btd,dp->btp', h, w_in)
    offs = np.cumsum(PROJ_SIZES)[:-1].tolist()
    qa, ka, va, qb, kb, vb = jnp.split(p, offs, axis=-1)
    heads = lambda x: x.reshape(b, t, -1, HEAD_DIM)
    return (_rope(heads(qa), pos), _rope(heads(ka), pos), heads(va),
            _rope(heads(qb), pos), _rope(heads(kb), pos), heads(vb))


def _merge_heads(oa, ob, g_out_a, g_out_b, w_out):
    o = jnp.concatenate([_rms_norm(oa, g_out_a), _rms_norm(ob, g_out_b)], axis=-1)
    return jnp.einsum('btm,md->btd', o, w_out)


def _last_rows(x, n):
    t = x.shape[1]
    if t < n:
        x = jnp.pad(x, ((0, 0), (n - t, 0)) + ((0, 0),) * (x.ndim - 2))
    return x[:, -n:]


def _hier_moe(h, w_rg, b_rg, w_re, b_re, w_gate, w_up, w_down):
    shp = h.shape
    hf = h.reshape(-1, D_MODEL)
    n = hf.shape[0]
    lg = jnp.einsum('nd,dg->ng', hf, w_rg, preferred_element_type=jnp.float32) + b_rg.astype(jnp.float32)
    grp = jnp.argmax(lg, axis=-1)
    p_grp = jnp.take_along_axis(jax.nn.softmax(lg, axis=-1), grp[:, None], axis=1)
    le = jnp.einsum('nd,de->ne', hf, w_re, preferred_element_type=jnp.float32) + b_re.astype(jnp.float32)
    le = jnp.take_along_axis(le.reshape(n, N_GROUPS, EXPERTS_PER_GROUP), grp[:, None, None], axis=1)[:, 0]
    top_val, top_idx = lax.top_k(le, TOP_K)
    gate_w = p_grp * jax.nn.softmax(top_val, axis=-1)
    expert = grp[:, None] * EXPERTS_PER_GROUP + top_idx
    s_tot = n * TOP_K
    e_flat = expert.reshape(-1).astype(jnp.int32)
    tok_flat = jnp.repeat(jnp.arange(n, dtype=jnp.int32), TOP_K)
    order = jnp.argsort(e_flat)
    e_s, tok_s = e_flat[order], tok_flat[order]
    w_s = gate_w.reshape(-1)[order]
    counts = jax.ops.segment_sum(jnp.ones_like(e_flat), e_flat, num_segments=N_EXPERTS)
    start = jnp.cumsum(counts) - counts
    padded = (counts + MOE_BLOCK - 1) // MOE_BLOCK * MOE_BLOCK
    pend = jnp.cumsum(padded)
    pstart = pend - padded
    dest = pstart[e_s] + jnp.arange(s_tot, dtype=jnp.int32) - start[e_s]
    n_blocks = (s_tot + N_EXPERTS * (MOE_BLOCK - 1) + MOE_BLOCK - 1) // MOE_BLOCK
    x_disp = jnp.zeros((n_blocks * MOE_BLOCK, D_MODEL), h.dtype).at[dest].set(hf[tok_s])
    blk_expert = jnp.minimum(jnp.searchsorted(pend, jnp.arange(n_blocks) * MOE_BLOCK, side='right'), N_EXPERTS - 1)

    def block_ffn(args):
        xb, e = args
        return (jax.nn.silu(xb @ w_gate[e]) * (xb @ w_up[e])) @ w_down[e]

    y_disp = lax.map(block_ffn, (x_disp.reshape(n_blocks, MOE_BLOCK, D_MODEL), blk_expert)).reshape(-1, D_MODEL)
    contrib = y_disp[dest] * w_s[:, None].astype(h.dtype)
    out = jax.ops.segment_sum(contrib, tok_s, num_segments=n)
    return out.reshape(shp)


def setup_inputs(seed: int = 0) -> dict:
    key = jax.random.key(seed)
    ks = jax.random.split(key, 21)
    nrm = lambda k, shape, scale: jax.random.normal(k, shape, jnp.float32) * scale
    gain = lambda k, shape: 1.0 + 0.05 * jax.random.normal(k, shape, jnp.float32)
    return {
        'x_prompt': nrm(ks[0], (BATCH, SEQ, D_MODEL), 1.0),
        'x_sample': nrm(ks[1], (DEC_BATCH, DEC_SEQ, D_MODEL), 1.0),
        'cache_a_k': nrm(ks[2], (DEPTH, DEC_BATCH, WINDOW_A, N_KV_A, HEAD_DIM), 1.0),
        'cache_a_v': nrm(ks[3], (DEPTH, DEC_BATCH, WINDOW_A, N_KV_A, HEAD_DIM), 1.0),
        'cache_b_k': nrm(ks[4], (DEPTH, DEC_BATCH, WINDOW_B, N_HEADS_B, HEAD_DIM), 1.0),
        'cache_b_v': nrm(ks[5], (DEPTH, DEC_BATCH, WINDOW_B, N_HEADS_B, HEAD_DIM), 1.0),
        'g_attn': gain(ks[6], (DEPTH, D_MODEL)),
        'w_in': nrm(ks[7], (DEPTH, D_MODEL, PROJ_WIDTH), D_MODEL ** -0.5),
        'attn_sinks': 4.0 + nrm(ks[8], (DEPTH, N_HEADS_A), 1.0),
        'g_out_a': gain(ks[9], (DEPTH, WIDTH_A)),
        'g_out_b': gain(ks[10], (DEPTH, WIDTH_B)),
        'w_out': nrm(ks[11], (DEPTH, MIX_WIDTH, D_MODEL), MIX_WIDTH ** -0.5),
        'g_ffn': gain(ks[12], (DEPTH, D_MODEL)),
        'w_router_group': nrm(ks[13], (DEPTH, D_MODEL, N_GROUPS), D_MODEL ** -0.5),
        'b_router_group': nrm(ks[14], (DEPTH, N_GROUPS), 0.01),
        'w_router_expert': nrm(ks[15], (DEPTH, D_MODEL, N_EXPERTS), D_MODEL ** -0.5),
        'b_router_expert': nrm(ks[16], (DEPTH, N_EXPERTS), 0.01),
        'w_gate': nrm(ks[17], (DEPTH, N_EXPERTS, D_MODEL, D_EXPERT), D_MODEL ** -0.5),
        'w_up': nrm(ks[18], (DEPTH, N_EXPERTS, D_MODEL, D_EXPERT), D_MODEL ** -0.5),
        'w_down': nrm(ks[19], (DEPTH, N_EXPERTS, D_EXPERT, D_MODEL), D_EXPERT ** -0.5),
        'g_final': gain(ks[20], (D_MODEL,)),
    }


def reference(x_prompt, x_sample, cache_a_k, cache_a_v, cache_b_k, cache_b_v, g_attn, w_in, attn_sinks,
              g_out_a, g_out_b, w_out, g_ffn, w_router_group, b_router_group, w_router_expert,
              b_router_expert, w_gate, w_up, w_down, g_final):
    xp, xs = x_prompt, x_sample
    pos_p = jnp.arange(x_prompt.shape[1])
    pos_s = PAST_LEN + jnp.arange(x_sample.shape[1])
    ak_p, av_p, bk_p, bv_p = [], [], [], []
    ak_s, av_s, bk_s, bv_s = [], [], [], []
    for l in range(DEPTH):
        qa, ka, va, qb, kb, vb = _project(_rms_norm(xp, g_attn[l]), w_in[l], pos_p)
        oa = _window_sink_prompt(qa, ka, va, attn_sinks[l])
        ob = _dilated_prompt(qb, kb, vb)
        xp = xp + _merge_heads(oa, ob, g_out_a[l], g_out_b[l], w_out[l])
        xp = xp + _hier_moe(_rms_norm(xp, g_ffn[l]), w_router_group[l], b_router_group[l],
                            w_router_expert[l], b_router_expert[l], w_gate[l], w_up[l], w_down[l])
        ak_p.append(_last_rows(ka, WINDOW_A))
        av_p.append(_last_rows(va, WINDOW_A))
        bk_p.append(_last_rows(kb, WINDOW_B))
        bv_p.append(_last_rows(vb, WINDOW_B))
        qa_s, ka_s, va_s, qb_s, kb_s, vb_s = _project(_rms_norm(xs, g_attn[l]), w_in[l], pos_s)
        kca = jnp.concatenate([cache_a_k[l], ka_s], axis=1)
        vca = jnp.concatenate([cache_a_v[l], va_s], axis=1)
        kcb = jnp.concatenate([cache_b_k[l], kb_s], axis=1)
        vcb = jnp.concatenate([cache_b_v[l], vb_s], axis=1)
        oa_s = _window_sink_sample(qa_s, kca, vca, attn_sinks[l])
        ob_s = _dilated_sample(qb_s, kcb, vcb)
        xs = xs + _merge_heads(oa_s, ob_s, g_out_a[l], g_out_b[l], w_out[l])
        xs = xs + _hier_moe(_rms_norm(xs, g_ffn[l]), w_router_group[l], b_router_group[l],
                            w_router_expert[l], b_router_expert[l], w_gate[l], w_up[l], w_down[l])
        ak_s.append(kca[:, -WINDOW_A:])
        av_s.append(vca[:, -WINDOW_A:])
        bk_s.append(kcb[:, -WINDOW_B:])
        bv_s.append(vcb[:, -WINDOW_B:])
    y_prompt = _rms_norm(xp, g_final)
    y_sample = _rms_norm(xs, g_final)
    return (y_prompt, y_sample,
            jnp.stack(ak_p), jnp.stack(av_p), jnp.stack(bk_p), jnp.stack(bv_p),
            jnp.stack(ak_s), jnp.stack(av_s), jnp.stack(bk_s), jnp.stack(bv_s))
```

```python
import functools

import jax
import jax.numpy as jnp
from jax import lax
from jax.experimental import pallas as pl
from jax.experimental.pallas import tpu as pltpu

D_MODEL = 2048
HEAD_DIM = 64
N_HEADS_A = 16
N_KV_A = 2
GROUP_A = N_HEADS_A // N_KV_A
N_HEADS_B = 16
WIDTH_A = N_HEADS_A * HEAD_DIM
WIDTH_B = N_HEADS_B * HEAD_DIM
KV_WIDTH_A = N_KV_A * HEAD_DIM
PROJ_WIDTH = WIDTH_A + 2 * KV_WIDTH_A + 3 * WIDTH_B
WINDOW_A = 128
DILATED_PAIRS = ((128, 1), (512, 4), (2048, 16))
WINDOW_B = 2048
BLOCK = 128
ROPE_THETA = 10000.0
N_GROUPS = 4
EXPERTS_PER_GROUP = 8
N_EXPERTS = N_GROUPS * EXPERTS_PER_GROUP
D_EXPERT = D_MODEL // 2
RMS_EPS = 1e-6
NEG_INF = -1e30
PAST_LEN = 16384

LANES = 128
VMEM_LIMIT = 56 * 1024 * 1024

COL_QA = 0
COL_QB = WIDTH_A
COL_KB = COL_QB + WIDTH_B
COL_VB = COL_KB + WIDTH_B
COL_KA = COL_VB + WIDTH_B
COL_VA = COL_KA + KV_WIDTH_A
KV_COLS = 2 * WIDTH_B + 2 * KV_WIDTH_A

PROJ_TM = 512
PROJ_TN = 256
RES = 16
ROW_TM = 256
MOE_TM = 256

_f32 = jnp.float32
_bf16 = jnp.bfloat16


def _cparams(sem):
    return pltpu.CompilerParams(dimension_semantics=sem, vmem_limit_bytes=VMEM_LIMIT)


def _lane_iota(shape):
    return lax.broadcasted_iota(jnp.int32, shape, len(shape) - 1)


def _rope_tables(pos):
    half = HEAD_DIM // 2
    inv_freq = ROPE_THETA ** (-jnp.arange(half, dtype=_f32) / half)
    ang = pos.astype(_f32)[:, None] * inv_freq[None, :]
    cos, sin = jnp.cos(ang), jnp.sin(ang)
    return jnp.tile(cos, (1, 4)), jnp.tile(jnp.concatenate([-sin, sin], axis=-1), (1, 2))


def _proj_kernel(x_ref, g_ref, w_ref, cos_ref, sin_ref, *refs, prompt):
    if prompt:
        p_ref, kv_ref, res_ref, h_scr, acc_scr = refs
    else:
        p_ref, h_scr = refs
    j = pl.program_id(2)

    @pl.when(j == 0)
    def _():
        x = x_ref[...]
        ms = jnp.mean(x * x, axis=-1, keepdims=True)
        h_scr[...] = (x * lax.rsqrt(ms + RMS_EPS) * g_ref[...]).astype(_bf16)

    acc = jnp.dot(h_scr[...], w_ref[...], preferred_element_type=_f32)
    lane = _lane_iota((1, LANES))
    first_half = (lane % HEAD_DIM) < (HEAD_DIM // 2)
    cos, sin = cos_ref[...], sin_ref[...]
    outs = []
    for half in range(PROJ_TN // LANES):
        sb = j * (PROJ_TN // LANES) + half
        is_v = ((sb >= COL_VB // LANES) & (sb < COL_KA // LANES)) | (sb >= COL_VA // LANES)
        a = acc[:, half * LANES:(half + 1) * LANES]
        swapped = jnp.where(first_half, pltpu.roll(a, LANES - HEAD_DIM // 2, axis=1),
                            pltpu.roll(a, HEAD_DIM // 2, axis=1))
        c = jnp.where(is_v, 1.0, cos)
        s = jnp.where(is_v, 0.0, sin)
        outs.append(a * c + swapped * s)
    y = jnp.concatenate(outs, axis=1)
    if not prompt:
        p_ref[...] = y
        return
    p_ref[...] = y.astype(_bf16)

    @pl.when(j >= COL_KB // PROJ_TN)
    def _():
        kv_ref[...] = y

    @pl.when((j >= COL_QB // PROJ_TN) & (j < COL_KA // PROJ_TN))
    def _():
        rows = PROJ_TM // RES
        for half in range(PROJ_TN // LANES):
            acc_scr[half] = outs[half]
        for c in range(RES):
            c16 = 4 * (c % 4) + c // 4
            for half in range(PROJ_TN // LANES):
                res_ref[c, :, half * LANES:(half + 1) * LANES] = (
                    acc_scr[half, pl.ds(c16, rows, stride=RES), :].astype(_bf16))


def _project_prompt(x, g, w_bf16, cos, sin):
    b, t, _ = x.shape
    nj = PROJ_WIDTH // PROJ_TN
    jkb, jqb = COL_KB // PROJ_TN, COL_QB // PROJ_TN
    n_res = 3 * WIDTH_B // PROJ_TN
    kv_map = lambda bi, i, j: (bi, i, jnp.maximum(j - jkb, 0))
    res_map = lambda bi, i, j: (bi, 0, i, jnp.clip(j - jqb, 0, n_res - 1))
    return pl.pallas_call(
        functools.partial(_proj_kernel, prompt=True),
        grid=(b, t // PROJ_TM, nj),
        in_specs=[
            pl.BlockSpec((None, PROJ_TM, D_MODEL), lambda bi, i, j: (bi, i, 0)),
            pl.BlockSpec((1, D_MODEL), lambda bi, i, j: (0, 0)),
            pl.BlockSpec((D_MODEL, PROJ_TN), lambda bi, i, j: (0, j)),
            pl.BlockSpec((PROJ_TM, LANES), lambda bi, i, j: (i, 0)),
            pl.BlockSpec((PROJ_TM, LANES), lambda bi, i, j: (i, 0)),
        ],
        out_specs=[
            pl.BlockSpec((None, PROJ_TM, PROJ_TN), lambda bi, i, j: (bi, i, j)),
            pl.BlockSpec((None, PROJ_TM, PROJ_TN), kv_map),
            pl.BlockSpec((None, RES, PROJ_TM // RES, PROJ_TN), res_map),
        ],
        out_shape=[
            jax.ShapeDtypeStruct((b, t, PROJ_WIDTH), _bf16),
            jax.ShapeDtypeStruct((b, t, KV_COLS), _f32),
            jax.ShapeDtypeStruct((b, RES, t // RES, 3 * WIDTH_B), _bf16),
        ],
        scratch_shapes=[pltpu.VMEM((PROJ_TM, D_MODEL), _bf16), pltpu.VMEM((PROJ_TN // LANES, PROJ_TM, LANES), _f32)],
        compiler_params=_cparams(("arbitrary", "arbitrary", "arbitrary")),
        name="proj_prompt",
    )(x, g, w_bf16, cos, sin)


def _project_sample(x, g, w_bf16, cos, sin):
    n = x.shape[0]
    return pl.pallas_call(
        functools.partial(_proj_kernel, prompt=False),
        grid=(1, n // PROJ_TM, PROJ_WIDTH // PROJ_TN),
        in_specs=[
            pl.BlockSpec((PROJ_TM, D_MODEL), lambda bi, i, j: (i, 0)),
            pl.BlockSpec((1, D_MODEL), lambda bi, i, j: (0, 0)),
            pl.BlockSpec((D_MODEL, PROJ_TN), lambda bi, i, j: (0, j)),
            pl.BlockSpec((PROJ_TM, LANES), lambda bi, i, j: (i, 0)),
            pl.BlockSpec((PROJ_TM, LANES), lambda bi, i, j: (i, 0)),
        ],
        out_specs=pl.BlockSpec((PROJ_TM, PROJ_TN), lambda bi, i, j: (i, j)),
        out_shape=jax.ShapeDtypeStruct((n, PROJ_WIDTH), _f32),
        scratch_shapes=[pltpu.VMEM((PROJ_TM, D_MODEL), _bf16)],
        compiler_params=_cparams(("arbitrary", "arbitrary", "arbitrary")),
        name="proj_sample",
    )(x, g, w_bf16, cos, sin)


def _half_masks():
    lane = _lane_iota((1, LANES))
    return lane < HEAD_DIM


def _dup_half(x, g):
    lo = _half_masks()
    r = pltpu.roll(x, HEAD_DIM, axis=x.ndim - 1)
    return jnp.where(lo, x, r) if g == 0 else jnp.where(lo, r, x)


def _band_attn_kernel(*refs, il, gqa):
    if gqa:
        sink_ref, q_ref, kp_ref, kc_ref, vp_ref, vc_ref, o_ref = refs
    else:
        q_ref, kp_ref, kc_ref, vp_ref, vc_ref, o_ref, lse_ref = refs
    blk = pl.program_id(2)
    q = q_ref[...].reshape(BLOCK, -1)
    k2 = jnp.concatenate([kp_ref[...].reshape(BLOCK, -1), kc_ref[...].reshape(BLOCK, -1)], axis=0)
    v2 = jnp.concatenate([vp_ref[...].reshape(BLOCK, -1), vc_ref[...].reshape(BLOCK, -1)], axis=0)

    per = BLOCK // il
    w = lambda idx: il * (idx % per) + idx // per
    qi = lax.broadcasted_iota(jnp.int32, (2 * BLOCK, 2 * BLOCK), 0) % BLOCK
    kj = lax.broadcasted_iota(jnp.int32, (2 * BLOCK, 2 * BLOCK), 1)
    cur = kj >= BLOCK
    dist = w(qi) - w(kj % BLOCK) + jnp.where(cur, 0, BLOCK)
    mask = (dist >= 0) & (dist <= BLOCK) & (cur | (blk > 0))

    lo = _half_masks()
    zero = jnp.zeros((), q.dtype)
    scale = HEAD_DIM ** -0.5
    if gqa:
        kf, vf = k2.astype(_f32), v2.astype(_f32)
        kdup = [_dup_half(kf, g).astype(_bf16) for g in range(N_KV_A)]
        vdup = [_dup_half(vf, g).astype(_bf16) for g in range(N_KV_A)]
    for hp in range(q.shape[1] // LANES):
        qp = q[:, hp * LANES:(hp + 1) * LANES]
        if gqa:
            kk = kdup[(2 * hp) // GROUP_A]
            vv = vdup[(2 * hp) // GROUP_A]
        else:
            kk = k2[:, hp * LANES:(hp + 1) * LANES]
            vv = v2[:, hp * LANES:(hp + 1) * LANES]
        qs = jnp.concatenate([jnp.where(lo, qp, zero), jnp.where(lo, zero, qp)], axis=0)
        s = lax.dot_general(qs, kk, (((1,), (1,)), ((), ())), preferred_element_type=_f32) * scale
        s = jnp.where(mask, s, NEG_INF)
        m = jnp.max(s, axis=-1, keepdims=True)
        p = jnp.exp(s - m)
        den = jnp.sum(p, axis=-1, keepdims=True)
        o2 = jnp.dot(p.astype(_bf16), vv, preferred_element_type=_f32) / den
        lse = m + jnp.log(den)
        if gqa:
            o2 = jnp.concatenate([
                o2[:BLOCK] * jax.nn.sigmoid(lse[:BLOCK] - sink_ref[2 * hp]),
                o2[BLOCK:] * jax.nn.sigmoid(lse[BLOCK:] - sink_ref[2 * hp + 1])], axis=0)
        o_ref[:, hp * LANES:(hp + 1) * LANES] = jnp.where(lo, o2[:BLOCK], o2[BLOCK:])
        if not gqa:
            lse_ref[:, hp * LANES:(hp + 1) * LANES] = jnp.where(
                lo, jnp.broadcast_to(lse[:BLOCK], (BLOCK, LANES)), jnp.broadcast_to(lse[BLOCK:], (BLOCK, LANES)))


def _band_attention(arr, n_seq, seq_len, row_block, col_blocks, *, il, sinks=None):
    gqa = sinks is not None
    b = arr.shape[0]
    nblk = seq_len // BLOCK
    wq = WIDTH_A if gqa else WIDTH_B
    wk = LANES if gqa else WIDTH_B
    cq, ck, cv = col_blocks
    assert len(arr.shape) == 3 + len(row_block)

    def spec(width, col, prev):
        def imap(bi, si, blk):
            r = jnp.maximum(blk - 1, 0) if prev else blk
            return (bi, si) + ((0, r) if len(row_block) == 2 else (r,)) + (col,)
        return pl.BlockSpec((None, None) + tuple(row_block) + (width,), imap)

    in_specs = [spec(wq, cq, False), spec(wk, ck, True), spec(wk, ck, False), spec(wk, cv, True), spec(wk, cv, False)]
    args = [arr] * 5
    if gqa:
        in_specs = [pl.BlockSpec(memory_space=pltpu.SMEM)] + in_specs
        args = [sinks] + args
    out_map = lambda bi, si, blk: (bi, si, blk, 0)
    n_out = 1 if gqa else 2
    outs = pl.pallas_call(
        functools.partial(_band_attn_kernel, il=il, gqa=gqa),
        grid=(b, n_seq, nblk),
        in_specs=in_specs,
        out_specs=[pl.BlockSpec((None, None, BLOCK, wq), out_map)] * n_out,
        out_shape=[jax.ShapeDtypeStruct((b, n_seq, seq_len, wq), _f32)] * n_out,
        compiler_params=_cparams(("arbitrary", "arbitrary", "arbitrary")),
        name="band_attn_a" if gqa else f"band_attn_b_il{il}_{n_seq}",
    )(*args)
    return outs


def _pad_rows(x, rows):
    return jnp.concatenate([x, jnp.zeros((rows - x.shape[0], x.shape[1]), x.dtype)], axis=0)


def _shift_in(cache, new_t, steps):
    lane = _lane_iota((1, LANES))
    keep = lane < LANES - steps
    ncol = cache.shape[1] // LANES
    cols = []
    prev = pltpu.roll(cache[:, :LANES], LANES - steps, axis=1)
    for c in range(ncol):
        nxt = new_t if c == ncol - 1 else pltpu.roll(cache[:, (c + 1) * LANES:(c + 2) * LANES], LANES - steps, axis=1)
        cols.append(jnp.where(keep, prev, nxt))
        prev = nxt
    return jnp.concatenate(cols, axis=1) if ncol > 1 else cols[0]


def _new_cols(new_rows, bi, s_len):
    t = jnp.transpose(_pad_rows(new_rows, LANES))
    return pltpu.roll(t, (LANES - s_len - bi * s_len) % LANES, axis=1)


def _sample_a_kernel(sink_ref, q_ref, kvn_ref, kc_ref, vc_ref, o_ref, ko_ref, vo_ref, *, s_len):
    nb = 8 // s_len
    q8 = q_ref[...]
    kn8 = kvn_ref[:, :LANES]
    vn8 = kvn_ref[:, LANES:]
    lo = _half_masks()
    row = lax.broadcasted_iota(jnp.int32, (8, 1), 0)
    scale = HEAD_DIM ** -0.5
    npair = GROUP_A // 2
    rows_g = npair * 16
    qrow = lax.broadcasted_iota(jnp.int32, (rows_g, LANES), 0) % 8
    lane = _lane_iota((rows_g, LANES))
    qb, qs_ = qrow // s_len, qrow % s_len
    out = jnp.zeros((8, WIDTH_A), _f32)
    for bi in range(nb):
        mine = (row // s_len) == bi
        kst = kc_ref[bi].reshape(N_KV_A * HEAD_DIM, WINDOW_A)
        vst = vc_ref[bi].reshape(N_KV_A * HEAD_DIM, WINDOW_A)
        ko_ref[bi] = _shift_in(kst, _new_cols(kn8, bi, s_len), s_len).reshape(N_KV_A, HEAD_DIM, WINDOW_A)
        vo_ref[bi] = _shift_in(vst, _new_cols(vn8, bi, s_len), s_len).reshape(N_KV_A, HEAD_DIM, WINDOW_A)
        kpos = PAST_LEN - WINDOW_A + lane
        d_c = PAST_LEN + qs_ - kpos
        mask_c = (d_c >= 0) & (d_c <= WINDOW_A) & (kpos >= 0)
        d_n = qs_ - lane % s_len
        mask_n = (lane < 8) & ((lane // s_len) == bi) & (d_n >= 0) & (d_n <= WINDOW_A)
        cols = []
        for g in range(N_KV_A):
            kdup = jnp.concatenate([kst[g * HEAD_DIM:(g + 1) * HEAD_DIM]] * 2, axis=0).astype(_bf16)
            vdup = jnp.concatenate([vst[g * HEAD_DIM:(g + 1) * HEAD_DIM]] * 2, axis=0).astype(_bf16)
            kn = _pad_rows(_dup_half(kn8, g), LANES).astype(_bf16)
            vn = _pad_rows(_dup_half(vn8, g), LANES).astype(_bf16)
            parts = []
            for pp in range(npair):
                hp = g * npair + pp
                qp = jnp.where(mine, q8[:, hp * LANES:(hp + 1) * LANES], 0.0)
                parts += [jnp.where(lo, qp, 0.0), jnp.where(lo, 0.0, qp)]
            qs = jnp.concatenate(parts, axis=0).astype(_bf16)
            s_c = jnp.dot(qs, kdup, preferred_element_type=_f32) * scale
            s_n = lax.dot_general(qs, kn, (((1,), (1,)), ((), ())), preferred_element_type=_f32) * scale
            s_c = jnp.where(mask_c, s_c, NEG_INF)
            s_n = jnp.where(mask_n, s_n, NEG_INF)
            m = jnp.maximum(jnp.max(s_c, axis=-1, keepdims=True), jnp.max(s_n, axis=-1, keepdims=True))
            p_c, p_n = jnp.exp(s_c - m), jnp.exp(s_n - m)
            den = jnp.sum(p_c, axis=-1, keepdims=True) + jnp.sum(p_n, axis=-1, keepdims=True)
            o = lax.dot_general(p_c.astype(_bf16), vdup, (((1,), (1,)), ((), ())), preferred_element_type=_f32)
            o = (o + jnp.dot(p_n.astype(_bf16), vn, preferred_element_type=_f32)) / den
            lse = m + jnp.log(den)
            for pp in range(npair):
                h0 = 2 * (g * npair + pp)
                r0 = pp * 16
                o0 = o[r0:r0 + 8] * jax.nn.sigmoid(lse[r0:r0 + 8] - sink_ref[h0])
                o1 = o[r0 + 8:r0 + 16] * jax.nn.sigmoid(lse[r0 + 8:r0 + 16] - sink_ref[h0 + 1])
                cols.append(jnp.where(lo, o0, o1))
        out = jnp.where(mine, jnp.concatenate(cols, axis=1), out)
    o_ref[...] = out


def _sample_attention_a(ps, kc_t, vc_t, sinks, s_len):
    n = ps.shape[0]
    nb = 8 // s_len
    bd = n // s_len
    cache_spec = pl.BlockSpec((nb, N_KV_A, HEAD_DIM, WINDOW_A), lambda i: (i, 0, 0, 0))
    return pl.pallas_call(
        functools.partial(_sample_a_kernel, s_len=s_len),
        grid=(n // 8,),
        in_specs=[
            pl.BlockSpec(memory_space=pltpu.SMEM),
            pl.BlockSpec((8, WIDTH_A), lambda i: (i, 0)),
            pl.BlockSpec((8, 2 * KV_WIDTH_A), lambda i: (i, COL_KA // (2 * KV_WIDTH_A))),
            cache_spec, cache_spec,
        ],
        out_specs=[pl.BlockSpec((8, WIDTH_A), lambda i: (i, 0)), cache_spec, cache_spec],
        out_shape=[
            jax.ShapeDtypeStruct((n, WIDTH_A), _f32),
            jax.ShapeDtypeStruct((bd, N_KV_A, HEAD_DIM, WINDOW_A), _f32),
            jax.ShapeDtypeStruct((bd, N_KV_A, HEAD_DIM, WINDOW_A), _f32),
        ],
        compiler_params=_cparams(("arbitrary",)),
        name="sample_attn_a",
    )(sinks, ps, ps, kc_t, vc_t)


def _sample_b_kernel(q_ref, kn_ref, vn_ref, kc_ref, vc_ref, o_ref, ko_ref, vo_ref, *, s_len):
    nb = 8 // s_len
    q8 = q_ref[...]
    kn8 = kn_ref[...]
    vn8 = vn_ref[...]
    lo = _half_masks()
    row = lax.broadcasted_iota(jnp.int32, (8, 1), 0)
    scale = HEAD_DIM ** -0.5
    qrow = lax.broadcasted_iota(jnp.int32, (16, 1), 0) % 8
    qs_ = qrow % s_len
    kn = _pad_rows(kn8, LANES).astype(_bf16)
    vn = _pad_rows(vn8, LANES).astype(_bf16)
    lane_n = _lane_iota((16, LANES))
    out = jnp.zeros((8, LANES), _f32)
    for bi in range(nb):
        mine = (row // s_len) == bi
        kst = kc_ref[bi].reshape(LANES, WINDOW_B)
        vst = vc_ref[bi].reshape(LANES, WINDOW_B)
        ko_ref[bi] = _shift_in(kst, _new_cols(kn8, bi, s_len), s_len).reshape(2, HEAD_DIM, WINDOW_B)
        vo_ref[bi] = _shift_in(vst, _new_cols(vn8, bi, s_len), s_len).reshape(2, HEAD_DIM, WINDOW_B)
        qp = jnp.where(mine, q8, 0.0)
        qs = jnp.concatenate([jnp.where(lo, qp, 0.0), jnp.where(lo, 0.0, qp)], axis=0).astype(_bf16)
        s_c = jnp.dot(qs, kst.astype(_bf16), preferred_element_type=_f32) * scale
        s_n = lax.dot_general(qs, kn, (((1,), (1,)), ((), ())), preferred_element_type=_f32) * scale
        d_n = qs_ - lane_n % s_len
        ok_n = (lane_n < 8) & ((lane_n // s_len) == bi) & (d_n >= 0)
        branches = []
        for w, r in DILATED_PAIRS:
            lo_i = WINDOW_B - w
            lane_c = lo_i + _lane_iota((16, w))
            d_c = WINDOW_B + qs_ - lane_c
            ok_c = (d_c % r == 0) & (d_c <= w) & (PAST_LEN - WINDOW_B + lane_c >= 0)
            sc = jnp.where(ok_c, s_c[:, lo_i:], NEG_INF)
            sn = jnp.where(ok_n & (d_n % r == 0) & (d_n <= w), s_n, NEG_INF)
            m = jnp.maximum(jnp.max(sc, axis=-1, keepdims=True), jnp.max(sn, axis=-1, keepdims=True))
            branches.append((lo_i, sc, sn, m))
        m_all = functools.reduce(jnp.maximum, [br[3] for br in branches])
        p_c = jnp.zeros((16, WINDOW_B), _f32)
        p_n = jnp.zeros((16, LANES), _f32)
        for lo_i, sc, sn, m in branches:
            e = jnp.exp(sc - m_all)
            p_c = p_c + (jnp.concatenate([jnp.zeros((16, lo_i), _f32), e], axis=1) if lo_i else e)
            p_n = p_n + jnp.exp(sn - m_all)
        den = jnp.sum(p_c, axis=-1, keepdims=True) + jnp.sum(p_n, axis=-1, keepdims=True)
        o = lax.dot_general(p_c.astype(_bf16), vst.astype(_bf16), (((1,), (1,)), ((), ())), preferred_element_type=_f32)
        o = (o + jnp.dot(p_n.astype(_bf16), vn, preferred_element_type=_f32)) / den
        out = jnp.where(mine, jnp.where(lo, o[:8], o[8:]), out)
    o_ref[...] = out


def _sample_attention_b(ps, kc_t, vc_t, s_len):
    n = ps.shape[0]
    nb = 8 // s_len
    bd = n // s_len
    npair = N_HEADS_B // 2
    cache_spec = pl.BlockSpec((nb, 2, HEAD_DIM, WINDOW_B), lambda i, hp: (i, hp, 0, 0))
    col = lambda base: (lambda i, hp: (i, base // LANES + hp))
    return pl.pallas_call(
        functools.partial(_sample_b_kernel, s_len=s_len),
        grid=(n // 8, npair),
        in_specs=[
            pl.BlockSpec((8, LANES), col(COL_QB)),
            pl.BlockSpec((8, LANES), col(COL_KB)),
            pl.BlockSpec((8, LANES), col(COL_VB)),
            cache_spec, cache_spec,
        ],
        out_specs=[pl.BlockSpec((8, LANES), lambda i, hp: (i, hp)), cache_spec, cache_spec],
        out_shape=[
            jax.ShapeDtypeStruct((n, WIDTH_B), _f32),
            jax.ShapeDtypeStruct((bd, N_HEADS_B, HEAD_DIM, WINDOW_B), _f32),
            jax.ShapeDtypeStruct((bd, N_HEADS_B, HEAD_DIM, WINDOW_B), _f32),
        ],
        compiler_params=_cparams(("arbitrary", "arbitrary")),
        name="sample_attn_b",
    )(ps, ps, ps, kc_t, vc_t)


def _rms(x, g):
    return x * lax.rsqrt(jnp.mean(x * x, axis=-1, keepdims=True) + RMS_EPS) * g


def _route(z):
    lane = _lane_iota(z.shape)
    big = jnp.int32(1 << 20)
    glane = lane < N_GROUPS
    gmax = jnp.max(jnp.where(glane, z, -jnp.inf), axis=-1, keepdims=True)
    grp = jnp.min(jnp.where(glane & (z == gmax), lane, big), axis=-1, keepdims=True)
    p_grp = 1.0 / jnp.sum(jnp.where(glane, jnp.exp(z - gmax), 0.0), axis=-1, keepdims=True)
    elane = (lane >= N_GROUPS) & (lane < N_GROUPS + N_EXPERTS) & ((lane - N_GROUPS) // EXPERTS_PER_GROUP == grp)
    t1 = jnp.max(jnp.where(elane, z, -jnp.inf), axis=-1, keepdims=True)
    i1 = jnp.min(jnp.where(elane & (z == t1), lane, big), axis=-1, keepdims=True)
    elane2 = elane & (lane != i1)
    t2 = jnp.max(jnp.where(elane2, z, -jnp.inf), axis=-1, keepdims=True)
    i2 = jnp.min(jnp.where(elane2 & (z == t2), lane, big), axis=-1, keepdims=True)
    e = jnp.exp(t2 - t1)
    w1 = p_grp * (1.0 / (1.0 + e))
    w2 = p_grp * (e / (1.0 + e))
    experts = jnp.where(lane == 0, i1 - N_GROUPS, jnp.where(lane == 1, i2 - N_GROUPS, 0))
    weights = jnp.where(lane == 0, w1, jnp.where(lane == 1, w2, 0.0))
    return experts, weights


def _merge_kernel(*refs, n_branch):
    n_lse = n_branch if n_branch > 1 else 0
    oa_ref = refs[0]
    ob_refs = refs[1:1 + n_branch]
    lse_refs = refs[1 + n_branch:1 + n_branch + n_lse]
    (x_ref, ga_ref, gb_ref, wo_ref, gf_ref, wr_ref, br_ref, _x1_in, _hp_in, _re_in, _rw_in,
     x1_ref, hp_ref, re_ref, rw_ref) = refs[1 + n_branch + n_lse:]
    if n_branch == 1:
        ob = ob_refs[0][...]
    else:
        lses = [r[...] for r in lse_refs]
        lmax = functools.reduce(jnp.maximum, lses)
        es = [jnp.exp(l - lmax) for l in lses]
        tot = functools.reduce(jnp.add, es)
        ob = functools.reduce(jnp.add, [(e / tot) * r[...] for e, r in zip(es, ob_refs)])
    o = jnp.concatenate([_rms(oa_ref[...], ga_ref[...]), _rms(ob, gb_ref[...])], axis=-1).astype(_bf16)
    x1 = x_ref[...] + jnp.dot(o, wo_ref[...], preferred_element_type=_f32)
    x1_ref[...] = x1
    h2 = _rms(x1, gf_ref[...])
    z = jnp.dot(h2, wr_ref[...], preferred_element_type=_f32, precision=lax.Precision.HIGHEST) + br_ref[...]
    experts, weights = _route(z)
    re_ref[...] = experts
    rw_ref[...] = weights
    bits = pltpu.bitcast(h2.astype(_bf16).astype(_f32), jnp.uint32)
    half = D_MODEL // 2
    hp_ref[...] = bits[:, :half] | (bits[:, half:] >> 16)


def _merge(oa, obs, lses, x, bufs, row0, ga, gb, wo_bf16, gf, wr, br):
    n = oa.shape[0]
    nb = len(obs)
    assert len(lses) == (nb if nb > 1 else 0)
    t0 = row0 // ROW_TM
    row = lambda w: pl.BlockSpec((ROW_TM, w), lambda i: (i, 0))
    full = lambda a: pl.BlockSpec(a.shape, lambda i: (0,) * a.ndim)
    anyspec = pl.BlockSpec(memory_space=pl.ANY)
    out_row = lambda w: pl.BlockSpec((ROW_TM, w), lambda i: (i + t0, 0))
    consts = [ga, gb, wo_bf16, gf, wr, br]
    return pl.pallas_call(
        functools.partial(_merge_kernel, n_branch=nb),
        grid=(n // ROW_TM,),
        in_specs=([row(WIDTH_A)] + [row(WIDTH_B)] * (nb + len(lses)) + [row(D_MODEL)] + [full(a) for a in consts]
                  + [anyspec] * 4),
        out_specs=[out_row(D_MODEL), out_row(D_MODEL // 2), out_row(LANES), out_row(LANES)],
        out_shape=[jax.ShapeDtypeStruct(a.shape, a.dtype) for a in bufs],
        input_output_aliases={2 + nb + len(lses) + len(consts) + k: k for k in range(4)},
        compiler_params=_cparams(("arbitrary",)),
        name=f"merge_{nb}",
    )(oa, *obs, *lses, x, *consts, *bufs)


def _rank_kernel(re_ref, rank_ref, cnt_ref, carry):
    @pl.when(pl.program_id(0) == 0)
    def _():
        carry[...] = jnp.zeros_like(carry)

    e = re_ref[...]
    lane = _lane_iota(e.shape)
    r_i = lax.broadcasted_iota(jnp.int32, (ROW_TM, ROW_TM), 0)
    c_i = lax.broadcasted_iota(jnp.int32, (ROW_TM, ROW_TM), 1)
    before = (c_i < r_i).astype(_bf16)
    base = carry[...]
    ranks = []
    for k in range(2):
        onehot = lane == e[:, k:k + 1]
        oh = onehot.astype(_f32)
        earlier = jnp.dot(before, onehot.astype(_bf16), preferred_element_type=_f32)
        ranks.append(jnp.sum(oh * (earlier + base), axis=-1, keepdims=True))
        base = base + jnp.sum(oh, axis=0, keepdims=True)
    carry[...] = base
    cnt_ref[...] = base
    rank_ref[...] = jnp.where(lane == 0, ranks[0], jnp.where(lane == 1, ranks[1], 0.0)).astype(jnp.int32)


def _rank(route_e):
    n = route_e.shape[0]
    return pl.pallas_call(
        _rank_kernel,
        grid=(n // ROW_TM,),
        in_specs=[pl.BlockSpec((ROW_TM, LANES), lambda i: (i, 0))],
        out_specs=[pl.BlockSpec((ROW_TM, LANES), lambda i: (i, 0)), pl.BlockSpec((1, LANES), lambda i: (0, 0))],
        out_shape=[jax.ShapeDtypeStruct((n, LANES), jnp.int32), jax.ShapeDtypeStruct((1, LANES), _f32)],
        scratch_shapes=[pltpu.VMEM((1, LANES), _f32)],
        compiler_params=_cparams(("arbitrary",)),
        name="moe_rank",
    )(route_e)


ROWS_PER_STEP = 512


def _row_copy(src, dst, i_src, i_dst, sem):
    return pltpu.make_async_copy(src.at[pl.ds(i_src, 1), :], dst.at[pl.ds(i_dst, 1), :], sem)


def _dispatch_kernel(dest_ref, src_ref, _init_ref, dst_ref, sem, *, gather):
    base = pl.program_id(0) * ROWS_PER_STEP

    def copy(p):
        tok = (base + p) // 2
        if gather:
            return _row_copy(src_ref, dst_ref.at[p % 2], dest_ref[p], tok, sem)
        return _row_copy(src_ref, dst_ref, tok, dest_ref[p], sem)

    def issue(p, c):
        copy(p).start()
        return c

    def drain(p, c):
        copy(p).wait()
        return c

    lax.fori_loop(0, ROWS_PER_STEP, issue, 0)
    lax.fori_loop(0, ROWS_PER_STEP, drain, 0)


def _move_rows(dest_flat, src, init, gather):
    return pl.pallas_call(
        functools.partial(_dispatch_kernel, gather=gather),
        grid=(dest_flat.shape[0] // ROWS_PER_STEP,),
        in_specs=[
            pl.BlockSpec((ROWS_PER_STEP,), lambda i: (i,), memory_space=pltpu.SMEM),
            pl.BlockSpec(memory_space=pl.ANY),
            pl.BlockSpec(memory_space=pl.ANY),
        ],
        out_specs=pl.BlockSpec(memory_space=pl.ANY),
        out_shape=jax.ShapeDtypeStruct(init.shape, init.dtype),
        input_output_aliases={2: 0},
        scratch_shapes=[pltpu.SemaphoreType.DMA],
        compiler_params=pltpu.CompilerParams(dimension_semantics=("arbitrary",), has_side_effects=True),
        name="moe_gather" if gather else "moe_dispatch",
    )(dest_flat, src, init)


def _ffn_kernel(te_ref, nt_ref, x_ref, wg_ref, wu_ref, wd_ref, y_ref):
    i = pl.program_id(0)

    @pl.when(i < nt_ref[0])
    def _():
        packed = x_ref[...]
        hi = pltpu.bitcast(packed & jnp.uint32(0xFFFF0000), _f32)
        lo = pltpu.bitcast(packed << 16, _f32)
        x = jnp.concatenate([hi, lo], axis=1).astype(_bf16)
        g = jnp.dot(x, wg_ref[...], preferred_element_type=_f32)
        u = jnp.dot(x, wu_ref[...], preferred_element_type=_f32)
        h = (jax.nn.silu(g) * u).astype(_bf16)
        y_ref[...] = jnp.dot(h, wd_ref[...], preferred_element_type=_f32)

    @pl.when(i >= nt_ref[0])
    def _():
        y_ref[...] = jnp.zeros_like(y_ref)


def _expert_ffn(tile_expert, n_tiles_used, x_disp, wg, wu, wd):
    rows = x_disp.shape[0]
    return pl.pallas_call(
        _ffn_kernel,
        grid_spec=pltpu.PrefetchScalarGridSpec(
            num_scalar_prefetch=2,
            grid=(rows // MOE_TM,),
            in_specs=[
                pl.BlockSpec((MOE_TM, D_MODEL // 2), lambda i, te, nt: (i, 0)),
                pl.BlockSpec((None, D_MODEL, D_EXPERT), lambda i, te, nt: (te[i], 0, 0)),
                pl.BlockSpec((None, D_MODEL, D_EXPERT), lambda i, te, nt: (te[i], 0, 0)),
                pl.BlockSpec((None, D_EXPERT, D_MODEL), lambda i, te, nt: (te[i], 0, 0)),
            ],
            out_specs=pl.BlockSpec((MOE_TM, D_MODEL), lambda i, te, nt: (i, 0)),
        ),
        out_shape=jax.ShapeDtypeStruct((rows, D_MODEL), _f32),
        compiler_params=_cparams(("arbitrary",)),
        name="moe_ffn",
    )(tile_expert, n_tiles_used, x_disp, wg, wu, wd)


def _final_kernel(x1_ref, y_ref, rw_ref, g_ref, o_ref):
    rw = rw_ref[...]
    moe = y_ref[0] * rw[:, 0:1] + y_ref[1] * rw[:, 1:2]
    o_ref[...] = _rms(x1_ref[...] + moe, g_ref[...])


def _final(x1, y2, rw, g, row0, n):
    t0 = row0 // ROW_TM
    return pl.pallas_call(
        _final_kernel,
        grid=(n // ROW_TM,),
        in_specs=[
            pl.BlockSpec((ROW_TM, D_MODEL), lambda i: (i + t0, 0)),
            pl.BlockSpec((2, ROW_TM, D_MODEL), lambda i: (0, i + t0, 0)),
            pl.BlockSpec((ROW_TM, LANES), lambda i: (i + t0, 0)),
            pl.BlockSpec((1, D_MODEL), lambda i: (0, 0)),
        ],
        out_specs=pl.BlockSpec((ROW_TM, D_MODEL), lambda i: (i, 0)),
        out_shape=jax.ShapeDtypeStruct((n, D_MODEL), _f32),
        compiler_params=_cparams(("arbitrary",)),
        name="moe_combine_final",
    )(x1, y2, rw, g)


def _moe_plan(route_e, rank, counts):
    counts = counts[0, :N_EXPERTS].astype(jnp.int32)
    padded = (counts + MOE_TM - 1) // MOE_TM * MOE_TM
    pend = jnp.cumsum(padded)
    pstart = pend - padded
    e = route_e[:, :2]
    dest = (pstart[e] + rank[:, :2]).reshape(-1)
    n_pairs = dest.shape[0]
    n_tiles = (n_pairs + N_EXPERTS * (MOE_TM - 1) + MOE_TM - 1) // MOE_TM
    tile_start = jnp.arange(n_tiles, dtype=jnp.int32) * MOE_TM
    tile_expert = jnp.minimum(jnp.searchsorted(pend, tile_start, side="right"), N_EXPERTS - 1).astype(jnp.int32)
    n_used = (pend[-1] // MOE_TM).astype(jnp.int32).reshape(1)
    return dest, tile_expert, n_used, n_tiles


def kernel(x_prompt, x_sample, cache_a_k, cache_a_v, cache_b_k, cache_b_v, g_attn, w_in, attn_sinks, g_out_a,
           g_out_b, w_out, g_ffn, w_router_group, b_router_group, w_router_expert, b_router_expert, w_gate, w_up,
           w_down, g_final):
    depth = g_attn.shape[0]
    assert depth == 1
    b, t, _ = x_prompt.shape
    bd, s_len, _ = x_sample.shape
    assert t % (RES * BLOCK) == 0 and t % PROJ_TM == 0 and 8 % s_len == 0
    n_p, n_s = b * t, bd * s_len
    n_tok = n_p + n_s
    l = 0

    w_l = w_in[l]
    w_in_b = jnp.concatenate([w_l[:, :WIDTH_A], w_l[:, WIDTH_A + 2 * KV_WIDTH_A:],
                              w_l[:, WIDTH_A:WIDTH_A + 2 * KV_WIDTH_A]], axis=1).astype(_bf16)
    w_out_b = w_out[l].astype(_bf16)
    wg_b, wu_b, wd_b = w_gate[l].astype(_bf16), w_up[l].astype(_bf16), w_down[l].astype(_bf16)
    pad = LANES - N_GROUPS - N_EXPERTS
    w_r = jnp.concatenate([w_router_group[l], w_router_expert[l], jnp.zeros((D_MODEL, pad), _f32)], axis=1)
    b_r = jnp.concatenate([b_router_group[l], b_router_expert[l], jnp.zeros((pad,), _f32)])[None]
    g_a, g_oa, g_ob, g_f = g_attn[l][None], g_out_a[l][None], g_out_b[l][None], g_ffn[l][None]
    sinks = attn_sinks[l]

    cos_p, sin_p = _rope_tables(jnp.arange(t))
    p_nat, kv_f32, p_res = _project_prompt(x_prompt, g_a, w_in_b, cos_p, sin_p)
    (oa,) = _band_attention(p_nat[:, None], 1, t, (BLOCK,), (0, COL_KA // LANES, COL_VA // LANES), il=1, sinks=sinks)
    cb = (COL_QB // WIDTH_B, COL_KB // WIDTH_B, COL_VB // WIDTH_B)
    o1, l1 = _band_attention(p_nat[:, None], 1, t, (BLOCK,), cb, il=1)
    o4, l4 = _band_attention(p_res.reshape(b, 4, 4, t // RES, 3 * WIDTH_B), 4, t // 4, (4, BLOCK // 4), (0, 1, 2), il=4)
    o16, l16 = _band_attention(p_res, RES, t // RES, (BLOCK,), (0, 1, 2), il=1)

    def from_res4(a):
        a = a.reshape(b, 4, t // RES // (BLOCK // 4), 4, BLOCK // 4, WIDTH_B)
        return a.transpose(0, 2, 4, 3, 1, 5).reshape(n_p, WIDTH_B)

    def from_res16(a):
        a = a.reshape(b, 4, 4, t // RES, WIDTH_B)
        return a.transpose(0, 3, 2, 1, 4).reshape(n_p, WIDTH_B)

    obs_p = [o1.reshape(n_p, WIDTH_B), from_res4(o4), from_res16(o16)]
    lses_p = [l1.reshape(n_p, WIDTH_B), from_res4(l4), from_res16(l16)]

    cos_s, sin_s = _rope_tables(PAST_LEN + jnp.arange(n_s) % s_len)
    ps = _project_sample(x_sample.reshape(n_s, D_MODEL), g_a, w_in_b, cos_s, sin_s)
    to_t = lambda c: jnp.transpose(c[l], (0, 2, 3, 1))
    from_t = lambda c: jnp.transpose(c, (0, 3, 1, 2))[None]
    oa_s, ak_s, av_s = _sample_attention_a(ps, to_t(cache_a_k), to_t(cache_a_v), sinks, s_len)
    ob_s, bk_s, bv_s = _sample_attention_b(ps, to_t(cache_b_k), to_t(cache_b_v), s_len)

    bufs = (jnp.zeros((n_tok, D_MODEL), _f32), jnp.zeros((n_tok, D_MODEL // 2), jnp.uint32),
            jnp.zeros((n_tok, LANES), jnp.int32), jnp.zeros((n_tok, LANES), _f32))
    consts = (g_oa, g_ob, w_out_b, g_f, w_r, b_r)
    bufs = _merge(oa.reshape(n_p, WIDTH_A), obs_p, lses_p, x_prompt.reshape(n_p, D_MODEL), bufs, 0, *consts)
    bufs = _merge(oa_s, [ob_s], [], x_sample.reshape(n_s, D_MODEL), bufs, n_p, *consts)
    x1, h2p, route_e, route_w = bufs

    rank, counts = _rank(route_e)
    dest, tile_expert, n_used, n_tiles = _moe_plan(route_e, rank, counts)
    x_disp = _move_rows(dest, h2p, jnp.zeros((n_tiles * MOE_TM, D_MODEL // 2), jnp.uint32), gather=False)
    y_disp = _expert_ffn(tile_expert, n_used, x_disp, wg_b, wu_b, wd_b)
    y2 = _move_rows(dest, y_disp, jnp.zeros((2, n_tok, D_MODEL), _f32), gather=True)
    g_fin = g_final[None]
    y_prompt = _final(x1, y2, route_w, g_fin, 0, n_p).reshape(b, t, D_MODEL)
    y_sample = _final(x1, y2, route_w, g_fin, n_p, n_s).reshape(bd, s_len, D_MODEL)

    kv = kv_f32
    heads = lambda a, h: a.reshape(b, -1, h, HEAD_DIM)[None]
    bk_p = heads(kv[:, t - WINDOW_B:, 0:WIDTH_B], N_HEADS_B)
    bv_p = heads(kv[:, t - WINDOW_B:, WIDTH_B:2 * WIDTH_B], N_HEADS_B)
    ak_p = heads(kv[:, t - WINDOW_A:, 2 * WIDTH_B:2 * WIDTH_B + KV_WIDTH_A], N_KV_A)
    av_p = heads(kv[:, t - WINDOW_A:, 2 * WIDTH_B + KV_WIDTH_A:], N_KV_A)
    return (y_prompt, y_sample, ak_p, av_p, bk_p, bv_p, from_t(ak_s), from_t(av_s), from_t(bk_s), from_t(bv_s))
```

```python
import functools

import jax
import jax.numpy as jnp
from jax import lax
from jax.experimental import pallas as pl
from jax.experimental.pallas import tpu as pltpu

D_MODEL = 2048
HEAD_DIM = 64
N_HEADS_A = 16
N_KV_A = 2
GROUP_A = N_HEADS_A // N_KV_A
N_HEADS_B = 16
WIDTH_A = N_HEADS_A * HEAD_DIM
WIDTH_B = N_HEADS_B * HEAD_DIM
KV_WIDTH_A = N_KV_A * HEAD_DIM
PROJ_WIDTH = WIDTH_A + 2 * KV_WIDTH_A + 3 * WIDTH_B
WINDOW_A = 128
DILATED_PAIRS = ((128, 1), (512, 4), (2048, 16))
WINDOW_B = 2048
BLOCK = 128
ROPE_THETA = 10000.0
N_GROUPS = 4
EXPERTS_PER_GROUP = 8
N_EXPERTS = N_GROUPS * EXPERTS_PER_GROUP
D_EXPERT = D_MODEL // 2
RMS_EPS = 1e-6
NEG_INF = -1e30
PAST_LEN = 16384

LANES = 128
VMEM_LIMIT = 56 * 1024 * 1024

COL_QA = 0
COL_QB = WIDTH_A
COL_KB = COL_QB + WIDTH_B
COL_VB = COL_KB + WIDTH_B
COL_KA = COL_VB + WIDTH_B
COL_VA = COL_KA + KV_WIDTH_A
KV_COLS = 2 * WIDTH_B + 2 * KV_WIDTH_A

PROJ_TM = 512
PROJ_TN = 256
RES = 16
ROW_TM = 256
MOE_TM = 256

_f32 = jnp.float32
_bf16 = jnp.bfloat16


def _cparams(sem):
    return pltpu.CompilerParams(dimension_semantics=sem, vmem_limit_bytes=VMEM_LIMIT)


def _lane_iota(shape):
    return lax.broadcasted_iota(jnp.int32, shape, len(shape) - 1)


def _rope_tables(pos):
    half = HEAD_DIM // 2
    inv_freq = ROPE_THETA ** (-jnp.arange(half, dtype=_f32) / half)
    ang = pos.astype(_f32)[:, None] * inv_freq[None, :]
    cos, sin = jnp.cos(ang), jnp.sin(ang)
    return jnp.tile(cos, (1, 4)), jnp.tile(jnp.concatenate([-sin, sin], axis=-1), (1, 2))


def _proj_kernel(x_ref, g_ref, w_ref, cos_ref, sin_ref, *refs, prompt):
    if prompt:
        p_ref, kv_ref, res_ref, h_scr, acc_scr = refs
    else:
        p_ref, h_scr = refs
    j = pl.program_id(2)

    @pl.when(j == 0)
    def _():
        x = x_ref[...]
        ms = jnp.mean(x * x, axis=-1, keepdims=True)
        h_scr[...] = (x * lax.rsqrt(ms + RMS_EPS) * g_ref[...]).astype(_bf16)

    lane = _lane_iota((1, LANES))
    first_half = (lane % HEAD_DIM) < (HEAD_DIM // 2)
    n_sub = 2
    sub = PROJ_TM // n_sub
    for r in range(n_sub):
        rs = slice(r * sub, (r + 1) * sub)
        acc = jnp.dot(h_scr[rs, :], w_ref[...], preferred_element_type=_f32)
        cos, sin = cos_ref[rs, :], sin_ref[rs, :]
        for half in range(PROJ_TN // LANES):
            cs = slice(half * LANES, (half + 1) * LANES)
            sb = j * (PROJ_TN // LANES) + half
            is_v = ((sb >= COL_VB // LANES) & (sb < COL_KA // LANES)) | (sb >= COL_VA // LANES)
            a = acc[:, cs]
            swapped = jnp.where(first_half, pltpu.roll(a, LANES - HEAD_DIM // 2, axis=1),
                                pltpu.roll(a, HEAD_DIM // 2, axis=1))
            y = a * jnp.where(is_v, 1.0, cos) + swapped * jnp.where(is_v, 0.0, sin)
            if prompt:
                p_ref[rs, cs] = y.astype(_bf16)
                kv_ref[rs, cs] = y
                acc_scr[half, rs, :] = y
            else:
                p_ref[rs, cs] = y
    if not prompt:
        return

    @pl.when((j >= COL_QB // PROJ_TN) & (j < COL_KA // PROJ_TN))
    def _():
        rows = PROJ_TM // RES
        for c in range(RES):
            c16 = 4 * (c % 4) + c // 4
            for half in range(PROJ_TN // LANES):
                res_ref[c, :, half * LANES:(half + 1) * LANES] = (
                    acc_scr[half, pl.ds(c16, rows, stride=RES), :].astype(_bf16))


def _project_prompt(x, g, w_bf16, cos, sin):
    b, t, _ = x.shape
    nj = PROJ_WIDTH // PROJ_TN
    jkb, jqb = COL_KB // PROJ_TN, COL_QB // PROJ_TN
    n_res = 3 * WIDTH_B // PROJ_TN
    kv_map = lambda bi, i, j: (bi, i, jnp.maximum(j - jkb, 0))
    res_map = lambda bi, i, j: (bi, 0, i, jnp.clip(j - jqb, 0, n_res - 1))
    return pl.pallas_call(
        functools.partial(_proj_kernel, prompt=True),
        grid=(b, t // PROJ_TM, nj),
        in_specs=[
            pl.BlockSpec((None, PROJ_TM, D_MODEL), lambda bi, i, j: (bi, i, 0)),
            pl.BlockSpec((1, D_MODEL), lambda bi, i, j: (0, 0)),
            pl.BlockSpec((D_MODEL, PROJ_TN), lambda bi, i, j: (0, j)),
            pl.BlockSpec((PROJ_TM, LANES), lambda bi, i, j: (i, 0)),
            pl.BlockSpec((PROJ_TM, LANES), lambda bi, i, j: (i, 0)),
        ],
        out_specs=[
            pl.BlockSpec((None, PROJ_TM, PROJ_TN), lambda bi, i, j: (bi, i, j)),
            pl.BlockSpec((None, PROJ_TM, PROJ_TN), kv_map),
            pl.BlockSpec((None, RES, PROJ_TM // RES, PROJ_TN), res_map),
        ],
        out_shape=[
            jax.ShapeDtypeStruct((b, t, PROJ_WIDTH), _bf16),
            jax.ShapeDtypeStruct((b, t, KV_COLS), _f32),
            jax.ShapeDtypeStruct((b, RES, t // RES, 3 * WIDTH_B), _bf16),
        ],
        scratch_shapes=[pltpu.VMEM((PROJ_TM, D_MODEL), _bf16), pltpu.VMEM((PROJ_TN // LANES, PROJ_TM, LANES), _f32)],
        compiler_params=_cparams(("arbitrary", "arbitrary", "arbitrary")),
        name="proj_prompt",
    )(x, g, w_bf16, cos, sin)


def _project_sample(x, g, w_bf16, cos, sin):
    n = x.shape[0]
    return pl.pallas_call(
        functools.partial(_proj_kernel, prompt=False),
        grid=(1, n // PROJ_TM, PROJ_WIDTH // PROJ_TN),
        in_specs=[
            pl.BlockSpec((PROJ_TM, D_MODEL), lambda bi, i, j: (i, 0)),
            pl.BlockSpec((1, D_MODEL), lambda bi, i, j: (0, 0)),
            pl.BlockSpec((D_MODEL, PROJ_TN), lambda bi, i, j: (0, j)),
            pl.BlockSpec((PROJ_TM, LANES), lambda bi, i, j: (i, 0)),
            pl.BlockSpec((PROJ_TM, LANES), lambda bi, i, j: (i, 0)),
        ],
        out_specs=pl.BlockSpec((PROJ_TM, PROJ_TN), lambda bi, i, j: (i, j)),
        out_shape=jax.ShapeDtypeStruct((n, PROJ_WIDTH), _f32),
        scratch_shapes=[pltpu.VMEM((PROJ_TM, D_MODEL), _bf16)],
        compiler_params=_cparams(("arbitrary", "arbitrary", "arbitrary")),
        name="proj_sample",
    )(x, g, w_bf16, cos, sin)


def _half_masks():
    lane = _lane_iota((1, LANES))
    return lane < HEAD_DIM


def _dup_half(x, g):
    lo = _half_masks()
    r = pltpu.roll(x, HEAD_DIM, axis=x.ndim - 1)
    return jnp.where(lo, x, r) if g == 0 else jnp.where(lo, r, x)


def _band_attn_kernel(*refs, il, gqa):
    if gqa:
        sink_ref, q_ref, kp_ref, kc_ref, vp_ref, vc_ref, o_ref = refs
    else:
        q_ref, kp_ref, kc_ref, vp_ref, vc_ref, o_ref, lse_ref = refs
    blk = pl.program_id(2)
    q = q_ref[...].reshape(BLOCK, -1)
    k2 = jnp.concatenate([kp_ref[...].reshape(BLOCK, -1), kc_ref[...].reshape(BLOCK, -1)], axis=0)
    v2 = jnp.concatenate([vp_ref[...].reshape(BLOCK, -1), vc_ref[...].reshape(BLOCK, -1)], axis=0)

    per = BLOCK // il
    w = lambda idx: il * (idx % per) + idx // per
    qi = lax.broadcasted_iota(jnp.int32, (2 * BLOCK, 2 * BLOCK), 0) % BLOCK
    kj = lax.broadcasted_iota(jnp.int32, (2 * BLOCK, 2 * BLOCK), 1)
    cur = kj >= BLOCK
    dist = w(qi) - w(kj % BLOCK) + jnp.where(cur, 0, BLOCK)
    mask = (dist >= 0) & (dist <= BLOCK) & (cur | (blk > 0))

    lo = _half_masks()
    zero = jnp.zeros((), q.dtype)
    scale = HEAD_DIM ** -0.5
    if gqa:
        kf, vf = k2.astype(_f32), v2.astype(_f32)
        kdup = [_dup_half(kf, g).astype(_bf16) for g in range(N_KV_A)]
        vdup = [_dup_half(vf, g).astype(_bf16) for g in range(N_KV_A)]
    for hp in range(q.shape[1] // LANES):
        qp = q[:, hp * LANES:(hp + 1) * LANES]
        if gqa:
            kk = kdup[(2 * hp) // GROUP_A]
            vv = vdup[(2 * hp) // GROUP_A]
        else:
            kk = k2[:, hp * LANES:(hp + 1) * LANES]
            vv = v2[:, hp * LANES:(hp + 1) * LANES]
        qs = jnp.concatenate([jnp.where(lo, qp, zero), jnp.where(lo, zero, qp)], axis=0)
        s = lax.dot_general(qs, kk, (((1,), (1,)), ((), ())), preferred_element_type=_f32) * scale
        s = jnp.where(mask, s, NEG_INF)
        m = jnp.max(s, axis=-1, keepdims=True)
        p = jnp.exp(s - m)
        den = jnp.sum(p, axis=-1, keepdims=True)
        o2 = jnp.dot(p.astype(_bf16), vv, preferred_element_type=_f32) / den
        lse = m + jnp.log(den)
        if gqa:
            o2 = jnp.concatenate([
                o2[:BLOCK] * jax.nn.sigmoid(lse[:BLOCK] - sink_ref[2 * hp]),
                o2[BLOCK:] * jax.nn.sigmoid(lse[BLOCK:] - sink_ref[2 * hp + 1])], axis=0)
        o_ref[:, hp * LANES:(hp + 1) * LANES] = jnp.where(lo, o2[:BLOCK], o2[BLOCK:])
        if not gqa:
            lse_ref[:, hp * LANES:(hp + 1) * LANES] = jnp.where(
                lo, jnp.broadcast_to(lse[:BLOCK], (BLOCK, LANES)), jnp.broadcast_to(lse[BLOCK:], (BLOCK, LANES)))


def _band_attention(arr, n_seq, seq_len, row_block, col_blocks, *, il, sinks=None):
    gqa = sinks is not None
    b = arr.shape[0]
    nblk = seq_len // BLOCK
    wq = WIDTH_A if gqa else WIDTH_B
    wk = LANES if gqa else WIDTH_B
    cq, ck, cv = col_blocks
    assert len(arr.shape) == 3 + len(row_block)

    def spec(width, col, prev):
        def imap(bi, si, blk):
            r = jnp.maximum(blk - 1, 0) if prev else blk
            return (bi, si) + ((0, r) if len(row_block) == 2 else (r,)) + (col,)
        return pl.BlockSpec((None, None) + tuple(row_block) + (width,), imap)

    in_specs = [spec(wq, cq, False), spec(wk, ck, True), spec(wk, ck, False), spec(wk, cv, True), spec(wk, cv, False)]
    args = [arr] * 5
    if gqa:
        in_specs = [pl.BlockSpec(memory_space=pltpu.SMEM)] + in_specs
        args = [sinks] + args
    out_map = lambda bi, si, blk: (bi, si, blk, 0)
    n_out = 1 if gqa else 2
    outs = pl.pallas_call(
        functools.partial(_band_attn_kernel, il=il, gqa=gqa),
        grid=(b, n_seq, nblk),
        in_specs=in_specs,
        out_specs=[pl.BlockSpec((None, None, BLOCK, wq), out_map)] * n_out,
        out_shape=[jax.ShapeDtypeStruct((b, n_seq, seq_len, wq), _f32)] * n_out,
        compiler_params=_cparams(("arbitrary", "arbitrary", "arbitrary")),
        name="band_attn_a" if gqa else f"band_attn_b_il{il}_{n_seq}",
    )(*args)
    return outs


def _pad_rows(x, rows):
    return jnp.concatenate([x, jnp.zeros((rows - x.shape[0], x.shape[1]), x.dtype)], axis=0)


def _shift_in(cache, new_t, steps):
    lane = _lane_iota((1, LANES))
    keep = lane < LANES - steps
    ncol = cache.shape[1] // LANES
    cols = []
    prev = pltpu.roll(cache[:, :LANES], LANES - steps, axis=1)
    for c in range(ncol):
        nxt = new_t if c == ncol - 1 else pltpu.roll(cache[:, (c + 1) * LANES:(c + 2) * LANES], LANES - steps, axis=1)
        cols.append(jnp.where(keep, prev, nxt))
        prev = nxt
    return jnp.concatenate(cols, axis=1) if ncol > 1 else cols[0]


def _new_cols(new_rows, bi, s_len):
    t = jnp.transpose(_pad_rows(new_rows, LANES))
    return pltpu.roll(t, (LANES - s_len - bi * s_len) % LANES, axis=1)


def _sample_a_kernel(sink_ref, q_ref, kvn_ref, kc_ref, vc_ref, o_ref, ko_ref, vo_ref, *, s_len):
    nb = 8 // s_len
    q8 = q_ref[...]
    kn8 = kvn_ref[:, :LANES]
    vn8 = kvn_ref[:, LANES:]
    lo = _half_masks()
    row = lax.broadcasted_iota(jnp.int32, (8, 1), 0)
    scale = HEAD_DIM ** -0.5
    npair = GROUP_A // 2
    rows_g = npair * 16
    qrow = lax.broadcasted_iota(jnp.int32, (rows_g, LANES), 0) % 8
    lane = _lane_iota((rows_g, LANES))
    qb, qs_ = qrow // s_len, qrow % s_len
    out = jnp.zeros((8, WIDTH_A), _f32)
    for bi in range(nb):
        mine = (row // s_len) == bi
        kst = kc_ref[bi].reshape(N_KV_A * HEAD_DIM, WINDOW_A)
        vst = vc_ref[bi].reshape(N_KV_A * HEAD_DIM, WINDOW_A)
        ko_ref[bi] = _shift_in(kst, _new_cols(kn8, bi, s_len), s_len).reshape(N_KV_A, HEAD_DIM, WINDOW_A)
        vo_ref[bi] = _shift_in(vst, _new_cols(vn8, bi, s_len), s_len).reshape(N_KV_A, HEAD_DIM, WINDOW_A)
        kpos = PAST_LEN - WINDOW_A + lane
        d_c = PAST_LEN + qs_ - kpos
        mask_c = (d_c >= 0) & (d_c <= WINDOW_A) & (kpos >= 0)
        d_n = qs_ - lane % s_len
        mask_n = (lane < 8) & ((lane // s_len) == bi) & (d_n >= 0) & (d_n <= WINDOW_A)
        cols = []
        for g in range(N_KV_A):
            kdup = jnp.concatenate([kst[g * HEAD_DIM:(g + 1) * HEAD_DIM]] * 2, axis=0).astype(_bf16)
            vdup = jnp.concatenate([vst[g * HEAD_DIM:(g + 1) * HEAD_DIM]] * 2, axis=0).astype(_bf16)
            kn = _pad_rows(_dup_half(kn8, g), LANES).astype(_bf16)
            vn = _pad_rows(_dup_half(vn8, g), LANES).astype(_bf16)
            parts = []
            for pp in range(npair):
                hp = g * npair + pp
                qp = jnp.where(mine, q8[:, hp * LANES:(hp + 1) * LANES], 0.0)
                parts += [jnp.where(lo, qp, 0.0), jnp.where(lo, 0.0, qp)]
            qs = jnp.concatenate(parts, axis=0).astype(_bf16)
            s_c = jnp.dot(qs, kdup, preferred_element_type=_f32) * scale
            s_n = lax.dot_general(qs, kn, (((1,), (1,)), ((), ())), preferred_element_type=_f32) * scale
            s_c = jnp.where(mask_c, s_c, NEG_INF)
            s_n = jnp.where(mask_n, s_n, NEG_INF)
            m = jnp.maximum(jnp.max(s_c, axis=-1, keepdims=True), jnp.max(s_n, axis=-1, keepdims=True))
            p_c, p_n = jnp.exp(s_c - m), jnp.exp(s_n - m)
            den = jnp.sum(p_c, axis=-1, keepdims=True) + jnp.sum(p_n, axis=-1, keepdims=True)
            o = lax.dot_general(p_c.astype(_bf16), vdup, (((1,), (1,)), ((), ())), preferred_element_type=_f32)
            o = (o + jnp.dot(p_n.astype(_bf16), vn, preferred_element_type=_f32)) / den
            lse = m + jnp.log(den)
            for pp in range(npair):
                h0 = 2 * (g * npair + pp)
                r0 = pp * 16
                o0 = o[r0:r0 + 8] * jax.nn.sigmoid(lse[r0:r0 + 8] - sink_ref[h0])
                o1 = o[r0 + 8:r0 + 16] * jax.nn.sigmoid(lse[r0 + 8:r0 + 16] - sink_ref[h0 + 1])
                cols.append(jnp.where(lo, o0, o1))
        out = jnp.where(mine, jnp.concatenate(cols, axis=1), out)
    o_ref[...] = out


def _sample_attention_a(ps, kc_t, vc_t, sinks, s_len):
    n = ps.shape[0]
    nb = 8 // s_len
    bd = n // s_len
    cache_spec = pl.BlockSpec((nb, N_KV_A, HEAD_DIM, WINDOW_A), lambda i: (i, 0, 0, 0))
    return pl.pallas_call(
        functools.partial(_sample_a_kernel, s_len=s_len),
        grid=(n // 8,),
        in_specs=[
            pl.BlockSpec(memory_space=pltpu.SMEM),
            pl.BlockSpec((8, WIDTH_A), lambda i: (i, 0)),
            pl.BlockSpec((8, 2 * KV_WIDTH_A), lambda i: (i, COL_KA // (2 * KV_WIDTH_A))),
            cache_spec, cache_spec,
        ],
        out_specs=[pl.BlockSpec((8, WIDTH_A), lambda i: (i, 0)), cache_spec, cache_spec],
        out_shape=[
            jax.ShapeDtypeStruct((n, WIDTH_A), _f32),
            jax.ShapeDtypeStruct((bd, N_KV_A, HEAD_DIM, WINDOW_A), _f32),
            jax.ShapeDtypeStruct((bd, N_KV_A, HEAD_DIM, WINDOW_A), _f32),
        ],
        compiler_params=_cparams(("arbitrary",)),
        name="sample_attn_a",
    )(sinks, ps, ps, kc_t, vc_t)


def _sample_b_kernel(q_ref, kn_ref, vn_ref, kc_ref, vc_ref, o_ref, ko_ref, vo_ref, *, s_len):
    nb = 8 // s_len
    q8 = q_ref[...]
    kn8 = kn_ref[...]
    vn8 = vn_ref[...]
    lo = _half_masks()
    row = lax.broadcasted_iota(jnp.int32, (8, 1), 0)
    scale = HEAD_DIM ** -0.5
    qrow = lax.broadcasted_iota(jnp.int32, (16, 1), 0) % 8
    qs_ = qrow % s_len
    kn = _pad_rows(kn8, LANES).astype(_bf16)
    vn = _pad_rows(vn8, LANES).astype(_bf16)
    lane_n = _lane_iota((16, LANES))
    out = jnp.zeros((8, LANES), _f32)
    for bi in range(nb):
        mine = (row // s_len) == bi
        kst = kc_ref[bi].reshape(LANES, WINDOW_B)
        vst = vc_ref[bi].reshape(LANES, WINDOW_B)
        ko_ref[bi] = _shift_in(kst, _new_cols(kn8, bi, s_len), s_len).reshape(2, HEAD_DIM, WINDOW_B)
        vo_ref[bi] = _shift_in(vst, _new_cols(vn8, bi, s_len), s_len).reshape(2, HEAD_DIM, WINDOW_B)
        qp = jnp.where(mine, q8, 0.0)
        qs = jnp.concatenate([jnp.where(lo, qp, 0.0), jnp.where(lo, 0.0, qp)], axis=0).astype(_bf16)
        s_c = jnp.dot(qs, kst.astype(_bf16), preferred_element_type=_f32) * scale
        s_n = lax.dot_general(qs, kn, (((1,), (1,)), ((), ())), preferred_element_type=_f32) * scale
        d_n = qs_ - lane_n % s_len
        ok_n = (lane_n < 8) & ((lane_n // s_len) == bi) & (d_n >= 0)
        branches = []
        for w, r in DILATED_PAIRS:
            lo_i = WINDOW_B - w
            lane_c = lo_i + _lane_iota((16, w))
            d_c = WINDOW_B + qs_ - lane_c
            ok_c = (d_c % r == 0) & (d_c <= w) & (PAST_LEN - WINDOW_B + lane_c >= 0)
            sc = jnp.where(ok_c, s_c[:, lo_i:], NEG_INF)
            sn = jnp.where(ok_n & (d_n % r == 0) & (d_n <= w), s_n, NEG_INF)
            m = jnp.maximum(jnp.max(sc, axis=-1, keepdims=True), jnp.max(sn, axis=-1, keepdims=True))
            branches.append((lo_i, sc, sn, m))
        m_all = functools.reduce(jnp.maximum, [br[3] for br in branches])
        p_c = jnp.zeros((16, WINDOW_B), _f32)
        p_n = jnp.zeros((16, LANES), _f32)
        for lo_i, sc, sn, m in branches:
            e = jnp.exp(sc - m_all)
            p_c = p_c + (jnp.concatenate([jnp.zeros((16, lo_i), _f32), e], axis=1) if lo_i else e)
            p_n = p_n + jnp.exp(sn - m_all)
        den = jnp.sum(p_c, axis=-1, keepdims=True) + jnp.sum(p_n, axis=-1, keepdims=True)
        o = lax.dot_general(p_c.astype(_bf16), vst.astype(_bf16), (((1,), (1,)), ((), ())), preferred_element_type=_f32)
        o = (o + jnp.dot(p_n.astype(_bf16), vn, preferred_element_type=_f32)) / den
        out = jnp.where(mine, jnp.where(lo, o[:8], o[8:]), out)
    o_ref[...] = out


def _sample_attention_b(ps, kc_t, vc_t, s_len):
    n = ps.shape[0]
    nb = 8 // s_len
    bd = n // s_len
    npair = N_HEADS_B // 2
    cache_spec = pl.BlockSpec((nb, 2, HEAD_DIM, WINDOW_B), lambda i, hp: (i, hp, 0, 0))
    col = lambda base: (lambda i, hp: (i, base // LANES + hp))
    return pl.pallas_call(
        functools.partial(_sample_b_kernel, s_len=s_len),
        grid=(n // 8, npair),
        in_specs=[
            pl.BlockSpec((8, LANES), col(COL_QB)),
            pl.BlockSpec((8, LANES), col(COL_KB)),
            pl.BlockSpec((8, LANES), col(COL_VB)),
            cache_spec, cache_spec,
        ],
        out_specs=[pl.BlockSpec((8, LANES), lambda i, hp: (i, hp)), cache_spec, cache_spec],
        out_shape=[
            jax.ShapeDtypeStruct((n, WIDTH_B), _f32),
            jax.ShapeDtypeStruct((bd, N_HEADS_B, HEAD_DIM, WINDOW_B), _f32),
            jax.ShapeDtypeStruct((bd, N_HEADS_B, HEAD_DIM, WINDOW_B), _f32),
        ],
        compiler_params=_cparams(("arbitrary", "arbitrary")),
        name="sample_attn_b",
    )(ps, ps, ps, kc_t, vc_t)


def _rms(x, g):
    return x * lax.rsqrt(jnp.mean(x * x, axis=-1, keepdims=True) + RMS_EPS) * g


def _route(z):
    lane = _lane_iota(z.shape)
    big = jnp.int32(1 << 20)
    glane = lane < N_GROUPS
    gmax = jnp.max(jnp.where(glane, z, -jnp.inf), axis=-1, keepdims=True)
    grp = jnp.min(jnp.where(glane & (z == gmax), lane, big), axis=-1, keepdims=True)
    p_grp = 1.0 / jnp.sum(jnp.where(glane, jnp.exp(z - gmax), 0.0), axis=-1, keepdims=True)
    elane = (lane >= N_GROUPS) & (lane < N_GROUPS + N_EXPERTS) & ((lane - N_GROUPS) // EXPERTS_PER_GROUP == grp)
    t1 = jnp.max(jnp.where(elane, z, -jnp.inf), axis=-1, keepdims=True)
    i1 = jnp.min(jnp.where(elane & (z == t1), lane, big), axis=-1, keepdims=True)
    elane2 = elane & (lane != i1)
    t2 = jnp.max(jnp.where(elane2, z, -jnp.inf), axis=-1, keepdims=True)
    i2 = jnp.min(jnp.where(elane2 & (z == t2), lane, big), axis=-1, keepdims=True)
    e = jnp.exp(t2 - t1)
    w1 = p_grp * (1.0 / (1.0 + e))
    w2 = p_grp * (e / (1.0 + e))
    experts = jnp.where(lane == 0, i1 - N_GROUPS, jnp.where(lane == 1, i2 - N_GROUPS, 0))
    weights = jnp.where(lane == 0, w1, jnp.where(lane == 1, w2, 0.0))
    return experts, weights


def _store_row_tiles(ref, val):
    t, c = val.shape
    per = c // LANES
    for j in range(per):
        ref[pl.ds(j, t, stride=per), :] = val[:, j * LANES:(j + 1) * LANES]


def _load_row_tiles(ref, t):
    per = ref.shape[0] // t
    return [ref[pl.ds(j, t, stride=per), :] for j in range(per)]


def _merge_kernel(*refs, n_branch):
    n_lse = n_branch if n_branch > 1 else 0
    oa_ref = refs[0]
    ob_refs = refs[1:1 + n_branch]
    lse_refs = refs[1 + n_branch:1 + n_branch + n_lse]
    x_ref, ga_ref, gb_ref, wo_ref, gf_ref, wr_ref, br_ref = refs[1 + n_branch + n_lse:8 + n_branch + n_lse]
    x1_ref, hp_ref, re_ref, rw_ref = refs[-4:]
    if n_branch == 1:
        ob = ob_refs[0][...]
    else:
        lses = [r[...] for r in lse_refs]
        lmax = functools.reduce(jnp.maximum, lses)
        es = [jnp.exp(l - lmax) for l in lses]
        tot = functools.reduce(jnp.add, es)
        ob = functools.reduce(jnp.add, [(e / tot) * r[...] for e, r in zip(es, ob_refs)])
    o = jnp.concatenate([_rms(oa_ref[...], ga_ref[...]), _rms(ob, gb_ref[...])], axis=-1).astype(_bf16)
    x1 = x_ref[...] + jnp.dot(o, wo_ref[...], preferred_element_type=_f32)
    x1_ref[...] = x1
    h2 = _rms(x1, gf_ref[...])
    h2b = h2.astype(_bf16)
    z = jnp.dot(h2b, wr_ref[...], preferred_element_type=_f32) + br_ref[...]
    experts, weights = _route(z)
    re_ref[...] = experts
    rw_ref[...] = weights
    bits = pltpu.bitcast(h2b.astype(_f32), jnp.uint32)
    half = D_MODEL // 2
    _store_row_tiles(hp_ref, bits[:, :half] | (bits[:, half:] >> 16))


PACK_TILES = D_MODEL // 2 // LANES
ROW_TILES = D_MODEL // LANES


def _merge(oa, obs, lses, x, ga, gb, wo_bf16, gf, wr_bf16, br):
    n = oa.shape[0]
    nb = len(obs)
    assert len(lses) == (nb if nb > 1 else 0)
    row = lambda r, w: pl.BlockSpec((r, w), lambda i: (i, 0))
    full = lambda a: pl.BlockSpec(a.shape, lambda i: (0,) * a.ndim)
    consts = [ga, gb, wo_bf16, gf, wr_bf16, br]
    return pl.pallas_call(
        functools.partial(_merge_kernel, n_branch=nb),
        grid=(n // ROW_TM,),
        in_specs=([row(ROW_TM, WIDTH_A)] + [row(ROW_TM, WIDTH_B)] * (nb + len(lses)) + [row(ROW_TM, D_MODEL)]
                  + [full(a) for a in consts]),
        out_specs=[row(ROW_TM, D_MODEL), row(ROW_TM * PACK_TILES, LANES), row(ROW_TM, LANES), row(ROW_TM, LANES)],
        out_shape=[jax.ShapeDtypeStruct((n, D_MODEL), _f32),
                   jax.ShapeDtypeStruct((n * PACK_TILES, LANES), jnp.uint32),
                   jax.ShapeDtypeStruct((n, LANES), jnp.int32),
                   jax.ShapeDtypeStruct((n, LANES), _f32)],
        compiler_params=_cparams(("arbitrary",)),
        name=f"merge_{nb}",
    )(oa, *obs, *lses, x, *consts)


def _rank_kernel(re_ref, rank_ref, cnt_ref, carry):
    @pl.when(pl.program_id(0) == 0)
    def _():
        carry[...] = jnp.zeros_like(carry)

    e = re_ref[...]
    lane = _lane_iota(e.shape)
    r_i = lax.broadcasted_iota(jnp.int32, (ROW_TM, ROW_TM), 0)
    c_i = lax.broadcasted_iota(jnp.int32, (ROW_TM, ROW_TM), 1)
    before = (c_i < r_i).astype(_bf16)
    base = carry[...]
    ranks = []
    for k in range(2):
        onehot = lane == e[:, k:k + 1]
        oh = onehot.astype(_f32)
        earlier = jnp.dot(before, onehot.astype(_bf16), preferred_element_type=_f32)
        ranks.append(jnp.sum(oh * (earlier + base), axis=-1, keepdims=True))
        base = base + jnp.sum(oh, axis=0, keepdims=True)
    carry[...] = base
    cnt_ref[...] = base
    rank_ref[...] = jnp.where(lane == 0, ranks[0], jnp.where(lane == 1, ranks[1], 0.0)).astype(jnp.int32)


def _rank(route_e):
    n = route_e.shape[0]
    return pl.pallas_call(
        _rank_kernel,
        grid=(n // ROW_TM,),
        in_specs=[pl.BlockSpec((ROW_TM, LANES), lambda i: (i, 0))],
        out_specs=[pl.BlockSpec((ROW_TM, LANES), lambda i: (i, 0)), pl.BlockSpec((1, LANES), lambda i: (0, 0))],
        out_shape=[jax.ShapeDtypeStruct((n, LANES), jnp.int32), jax.ShapeDtypeStruct((1, LANES), _f32)],
        scratch_shapes=[pltpu.VMEM((1, LANES), _f32)],
        compiler_params=_cparams(("arbitrary",)),
        name="moe_rank",
    )(route_e)


ROWS_PER_STEP = 512


def _row_copy(src, dst, i_src, i_dst, per, sem):
    return pltpu.make_async_copy(src.at[pl.ds(pl.multiple_of(i_src * per, per), per), :],
                                 dst.at[pl.ds(pl.multiple_of(i_dst * per, per), per), :], sem)


def _dispatch_kernel(dest_ref, src_ref, *refs, gather):
    dst_ref, sem = refs[-2:]
    base = pl.program_id(0) * ROWS_PER_STEP

    def copy(p):
        tok = (base + p) // 2
        if gather:
            return _row_copy(src_ref, dst_ref.at[p % 2], dest_ref[p], tok, ROW_TILES, sem)
        return _row_copy(src_ref, dst_ref, tok, dest_ref[p], PACK_TILES, sem)

    def issue(p, c):
        copy(p).start()
        return c

    def drain(p, c):
        copy(p).wait()
        return c

    lax.fori_loop(0, ROWS_PER_STEP, issue, 0)
    lax.fori_loop(0, ROWS_PER_STEP, drain, 0)


def _move_rows(dest_flat, src, out, gather):
    aliased = not isinstance(out, jax.ShapeDtypeStruct)
    anyspec = pl.BlockSpec(memory_space=pl.ANY)
    return pl.pallas_call(
        functools.partial(_dispatch_kernel, gather=gather),
        grid=(dest_flat.shape[0] // ROWS_PER_STEP,),
        in_specs=[pl.BlockSpec((ROWS_PER_STEP,), lambda i: (i,), memory_space=pltpu.SMEM), anyspec]
        + ([anyspec] if aliased else []),
        out_specs=anyspec,
        out_shape=jax.ShapeDtypeStruct(out.shape, out.dtype),
        input_output_aliases={2: 0} if aliased else {},
        scratch_shapes=[pltpu.SemaphoreType.DMA],
        compiler_params=pltpu.CompilerParams(dimension_semantics=("arbitrary",), has_side_effects=True),
        name="moe_gather" if gather else "moe_dispatch",
    )(dest_flat, src, *((out,) if aliased else ()))


def _ffn_kernel(te_ref, nt_ref, x_ref, wg_ref, wu_ref, wd_ref, y_ref):
    i = pl.program_id(0)

    @pl.when(i < nt_ref[0])
    def _():
        packed = _load_row_tiles(x_ref, MOE_TM)
        hi = [pltpu.bitcast(p & jnp.uint32(0xFFFF0000), _f32) for p in packed]
        lo = [pltpu.bitcast(p << 16, _f32) for p in packed]
        x = jnp.concatenate(hi + lo, axis=1).astype(_bf16)
        g = jnp.dot(x, wg_ref[...], preferred_element_type=_f32)
        u = jnp.dot(x, wu_ref[...], preferred_element_type=_f32)
        h = (jax.nn.silu(g) * u).astype(_bf16)
        _store_row_tiles(y_ref, jnp.dot(h, wd_ref[...], preferred_element_type=_f32))

    @pl.when(i >= nt_ref[0])
    def _():
        y_ref[...] = jnp.zeros_like(y_ref)


def _expert_ffn(tile_expert, n_tiles_used, x_disp, wg, wu, wd):
    rows = x_disp.shape[0] // PACK_TILES
    return pl.pallas_call(
        _ffn_kernel,
        grid_spec=pltpu.PrefetchScalarGridSpec(
            num_scalar_prefetch=2,
            grid=(rows // MOE_TM,),
            in_specs=[
                pl.BlockSpec((MOE_TM * PACK_TILES, LANES), lambda i, te, nt: (i, 0)),
                pl.BlockSpec((None, D_MODEL, D_EXPERT), lambda i, te, nt: (te[i], 0, 0)),
                pl.BlockSpec((None, D_MODEL, D_EXPERT), lambda i, te, nt: (te[i], 0, 0)),
                pl.BlockSpec((None, D_EXPERT, D_MODEL), lambda i, te, nt: (te[i], 0, 0)),
            ],
            out_specs=pl.BlockSpec((MOE_TM * ROW_TILES, LANES), lambda i, te, nt: (i, 0)),
        ),
        out_shape=jax.ShapeDtypeStruct((rows * ROW_TILES, LANES), _f32),
        compiler_params=_cparams(("arbitrary",)),
        name="moe_ffn",
    )(tile_expert, n_tiles_used, x_disp, wg, wu, wd)


def _final_kernel(x1_ref, y_ref, rw_ref, g_ref, o_ref):
    rw = rw_ref[...]
    y0 = jnp.concatenate(_load_row_tiles(y_ref.at[0], ROW_TM), axis=1)
    y1 = jnp.concatenate(_load_row_tiles(y_ref.at[1], ROW_TM), axis=1)
    moe = y0 * rw[:, 0:1] + y1 * rw[:, 1:2]
    o_ref[...] = _rms(x1_ref[...] + moe, g_ref[...])


def _final(x1, y2, rw, g, row0, n):
    t0 = row0 // ROW_TM
    return pl.pallas_call(
        _final_kernel,
        grid=(n // ROW_TM,),
        in_specs=[
            pl.BlockSpec((ROW_TM, D_MODEL), lambda i: (i + t0, 0)),
            pl.BlockSpec((2, ROW_TM * ROW_TILES, LANES), lambda i: (0, i + t0, 0)),
            pl.BlockSpec((ROW_TM, LANES), lambda i: (i + t0, 0)),
            pl.BlockSpec((1, D_MODEL), lambda i: (0, 0)),
        ],
        out_specs=pl.BlockSpec((ROW_TM, D_MODEL), lambda i: (i, 0)),
        out_shape=jax.ShapeDtypeStruct((n, D_MODEL), _f32),
        compiler_params=_cparams(("arbitrary",)),
        name="moe_combine_final",
    )(x1, y2, rw, g)


def _moe_plan(route_e, rank, counts):
    counts = counts[0, :N_EXPERTS].astype(jnp.int32)
    padded = (counts + MOE_TM - 1) // MOE_TM * MOE_TM
    pend = jnp.cumsum(padded)
    pstart = pend - padded
    e = route_e[:, :2]
    dest = (pstart[e] + rank[:, :2]).reshape(-1)
    n_pairs = dest.shape[0]
    n_tiles = (n_pairs + N_EXPERTS * (MOE_TM - 1) + MOE_TM - 1) // MOE_TM
    tile_start = jnp.arange(n_tiles, dtype=jnp.int32) * MOE_TM
    tile_expert = jnp.minimum(jnp.sum(pend[None, :] <= tile_start[:, None], axis=1), N_EXPERTS - 1).astype(jnp.int32)
    n_used = (pend[-1] // MOE_TM).astype(jnp.int32).reshape(1)
    return dest, tile_expert, n_used, n_tiles


def kernel(x_prompt, x_sample, cache_a_k, cache_a_v, cache_b_k, cache_b_v, g_attn, w_in, attn_sinks, g_out_a,
           g_out_b, w_out, g_ffn, w_router_group, b_router_group, w_router_expert, b_router_expert, w_gate, w_up,
           w_down, g_final):
    depth = g_attn.shape[0]
    assert depth == 1
    b, t, _ = x_prompt.shape
    bd, s_len, _ = x_sample.shape
    assert t % (RES * BLOCK) == 0 and t % PROJ_TM == 0 and 8 % s_len == 0
    n_p, n_s = b * t, bd * s_len
    n_tok = n_p + n_s
    l = 0

    w_l = w_in[l]
    w_in_b = jnp.concatenate([w_l[:, :WIDTH_A], w_l[:, WIDTH_A + 2 * KV_WIDTH_A:],
                              w_l[:, WIDTH_A:WIDTH_A + 2 * KV_WIDTH_A]], axis=1).astype(_bf16)
    w_out_b = w_out[l].astype(_bf16)
    wg_b, wu_b, wd_b = w_gate[l].astype(_bf16), w_up[l].astype(_bf16), w_down[l].astype(_bf16)
    pad = LANES - N_GROUPS - N_EXPERTS
    w_r = jnp.concatenate([w_router_group[l], w_router_expert[l], jnp.zeros((D_MODEL, pad), _f32)], axis=1)
    b_r = jnp.concatenate([b_router_group[l], b_router_expert[l], jnp.zeros((pad,), _f32)])[None]
    g_a, g_oa, g_ob, g_f = g_attn[l][None], g_out_a[l][None], g_out_b[l][None], g_ffn[l][None]
    sinks = attn_sinks[l]

    cos_p, sin_p = _rope_tables(jnp.arange(t))
    p_nat, kv_f32, p_res = _project_prompt(x_prompt, g_a, w_in_b, cos_p, sin_p)
    (oa,) = _band_attention(p_nat[:, None], 1, t, (BLOCK,), (0, COL_KA // LANES, COL_VA // LANES), il=1, sinks=sinks)
    cb = (COL_QB // WIDTH_B, COL_KB // WIDTH_B, COL_VB // WIDTH_B)
    o1, l1 = _band_attention(p_nat[:, None], 1, t, (BLOCK,), cb, il=1)
    o4, l4 = _band_attention(p_res.reshape(b, 4, 4, t // RES, 3 * WIDTH_B), 4, t // 4, (4, BLOCK // 4), (0, 1, 2), il=4)
    o16, l16 = _band_attention(p_res, RES, t // RES, (BLOCK,), (0, 1, 2), il=1)

    def from_res4(a):
        a = a.reshape(b, 4, t // RES // (BLOCK // 4), 4, BLOCK // 4, WIDTH_B)
        return a.transpose(0, 2, 4, 3, 1, 5).reshape(n_p, WIDTH_B)

    def from_res16(a):
        a = a.reshape(b, 4, 4, t // RES, WIDTH_B)
        return a.transpose(0, 3, 2, 1, 4).reshape(n_p, WIDTH_B)

    obs_p = [o1.reshape(n_p, WIDTH_B), from_res4(o4), from_res16(o16)]
    lses_p = [l1.reshape(n_p, WIDTH_B), from_res4(l4), from_res16(l16)]

    cos_s, sin_s = _rope_tables(PAST_LEN + jnp.arange(n_s) % s_len)
    ps = _project_sample(x_sample.reshape(n_s, D_MODEL), g_a, w_in_b, cos_s, sin_s)
    to_t = lambda c: jnp.transpose(c[l], (0, 2, 3, 1))
    from_t = lambda c: jnp.transpose(c, (0, 3, 1, 2))[None]
    oa_s, ak_s, av_s = _sample_attention_a(ps, to_t(cache_a_k), to_t(cache_a_v), sinks, s_len)
    ob_s, bk_s, bv_s = _sample_attention_b(ps, to_t(cache_b_k), to_t(cache_b_v), s_len)

    consts = (g_oa, g_ob, w_out_b, g_f, w_r.astype(_bf16), b_r)
    x1_p, h2_p, re_p, rw_p = _merge(oa.reshape(n_p, WIDTH_A), obs_p, lses_p, x_prompt.reshape(n_p, D_MODEL), *consts)
    x1_s, h2_s, re_s, rw_s = _merge(oa_s, [ob_s], [], x_sample.reshape(n_s, D_MODEL), *consts)

    route_e = jnp.concatenate([re_p, re_s], axis=0)
    rank, counts = _rank(route_e)
    dest, tile_expert, n_used, n_tiles = _moe_plan(route_e, rank, counts)
    dest_p, dest_s = dest[:2 * n_p], dest[2 * n_p:]
    x_disp = jnp.zeros((n_tiles * MOE_TM * PACK_TILES, LANES), jnp.uint32)
    x_disp = _move_rows(dest_p, h2_p, x_disp, gather=False)
    x_disp = _move_rows(dest_s, h2_s, x_disp, gather=False)
    y_disp = _expert_ffn(tile_expert, n_used, x_disp, wg_b, wu_b, wd_b)
    y2_p = _move_rows(dest_p, y_disp, jax.ShapeDtypeStruct((2, n_p * ROW_TILES, LANES), _f32), gather=True)
    y2_s = _move_rows(dest_s, y_disp, jax.ShapeDtypeStruct((2, n_s * ROW_TILES, LANES), _f32), gather=True)
    g_fin = g_final[None]
    y_prompt = _final(x1_p, y2_p, rw_p, g_fin, 0, n_p).reshape(b, t, D_MODEL)
    y_sample = _final(x1_s, y2_s, rw_s, g_fin, 0, n_s).reshape(bd, s_len, D_MODEL)

    kv = kv_f32
    heads = lambda a, h: a.reshape(b, -1, h, HEAD_DIM)[None]
    bk_p = heads(kv[:, t - WINDOW_B:, 0:WIDTH_B], N_HEADS_B)
    bv_p = heads(kv[:, t - WINDOW_B:, WIDTH_B:2 * WIDTH_B], N_HEADS_B)
    ak_p = heads(kv[:, t - WINDOW_A:, 2 * WIDTH_B:2 * WIDTH_B + KV_WIDTH_A], N_KV_A)
    av_p = heads(kv[:, t - WINDOW_A:, 2 * WIDTH_B + KV_WIDTH_A:], N_KV_A)
    return (y_prompt, y_sample, ak_p, av_p, bk_p, bv_p, from_t(ak_s), from_t(av_s), from_t(bk_s), from_t(bv_s))
```

```python
import functools

import jax
import jax.numpy as jnp
from jax import lax
from jax.experimental import pallas as pl
from jax.experimental.pallas import tpu as pltpu

D_MODEL = 2048
HEAD_DIM = 64
N_HEADS_A = 16
N_KV_A = 2
GROUP_A = N_HEADS_A // N_KV_A
N_HEADS_B = 16
WIDTH_A = N_HEADS_A * HEAD_DIM
WIDTH_B = N_HEADS_B * HEAD_DIM
KV_WIDTH_A = N_KV_A * HEAD_DIM
PROJ_WIDTH = WIDTH_A + 2 * KV_WIDTH_A + 3 * WIDTH_B
WINDOW_A = 128
DILATED_PAIRS = ((128, 1), (512, 4), (2048, 16))
WINDOW_B = 2048
BLOCK = 128
ROPE_THETA = 10000.0
N_GROUPS = 4
EXPERTS_PER_GROUP = 8
N_EXPERTS = N_GROUPS * EXPERTS_PER_GROUP
D_EXPERT = D_MODEL // 2
RMS_EPS = 1e-6
NEG_INF = -1e30
PAST_LEN = 16384

LANES = 128
VMEM_LIMIT = 56 * 1024 * 1024

COL_QA = 0
COL_QB = WIDTH_A
COL_KB = COL_QB + WIDTH_B
COL_VB = COL_KB + WIDTH_B
COL_KA = COL_VB + WIDTH_B
COL_VA = COL_KA + KV_WIDTH_A
KV_COLS = 2 * WIDTH_B + 2 * KV_WIDTH_A

PROJ_TM = 512
PROJ_TN = 256
RES = 16
ROW_TM = 256
MOE_TM = 256

_f32 = jnp.float32
_bf16 = jnp.bfloat16


def _cparams(sem):
    return pltpu.CompilerParams(dimension_semantics=sem, vmem_limit_bytes=VMEM_LIMIT)


def _lane_iota(shape):
    return lax.broadcasted_iota(jnp.int32, shape, len(shape) - 1)


def _rope_tables(pos):
    half = HEAD_DIM // 2
    inv_freq = ROPE_THETA ** (-jnp.arange(half, dtype=_f32) / half)
    ang = pos.astype(_f32)[:, None] * inv_freq[None, :]
    cos, sin = jnp.cos(ang), jnp.sin(ang)
    return jnp.tile(cos, (1, 4)), jnp.tile(jnp.concatenate([-sin, sin], axis=-1), (1, 2))


def _proj_kernel(x_ref, g_ref, w_ref, cos_ref, sin_ref, *refs, prompt):
    if prompt:
        p_ref, kv_ref, res_ref, h_scr, mm_scr, rope_scr = refs
    else:
        p_ref, h_scr, mm_scr = refs
    j = pl.program_id(2)
    jj = j - 1

    @pl.when(j == 0)
    def _():
        x = x_ref[...]
        ms = jnp.mean(x * x, axis=-1, keepdims=True)
        h_scr[...] = (x * lax.rsqrt(ms + RMS_EPS) * g_ref[...]).astype(_bf16)
        mm_scr[...] = jnp.zeros_like(mm_scr)

    lane = _lane_iota((1, LANES))
    first_half = (lane % HEAD_DIM) < (HEAD_DIM // 2)
    cos, sin = cos_ref[...], sin_ref[...]
    for half in range(PROJ_TN // LANES):
        cs = slice(half * LANES, (half + 1) * LANES)
        sb = jj * (PROJ_TN // LANES) + half
        is_v = ((sb >= COL_VB // LANES) & (sb < COL_KA // LANES)) | (sb >= COL_VA // LANES)
        a = mm_scr[:, cs]
        swapped = jnp.where(first_half, pltpu.roll(a, LANES - HEAD_DIM // 2, axis=1),
                            pltpu.roll(a, HEAD_DIM // 2, axis=1))
        y = a * jnp.where(is_v, 1.0, cos) + swapped * jnp.where(is_v, 0.0, sin)
        if prompt:
            p_ref[:, cs] = y.astype(_bf16)
            kv_ref[:, cs] = y
            rope_scr[half] = y
        else:
            p_ref[:, cs] = y
    mm_scr[...] = jnp.dot(h_scr[...], w_ref[...], preferred_element_type=_f32)
    if not prompt:
        return

    @pl.when((jj >= COL_QB // PROJ_TN) & (jj < COL_KA // PROJ_TN))
    def _():
        rows = PROJ_TM // RES
        for c in range(RES):
            c16 = 4 * (c % 4) + c // 4
            for half in range(PROJ_TN // LANES):
                res_ref[c, :, half * LANES:(half + 1) * LANES] = (
                    rope_scr[half, pl.ds(c16, rows, stride=RES), :].astype(_bf16))


PROJ_NJ = PROJ_WIDTH // PROJ_TN


def _proj_in_specs(x_spec):
    return [
        x_spec,
        pl.BlockSpec((1, D_MODEL), lambda bi, i, j: (0, 0)),
        pl.BlockSpec((D_MODEL, PROJ_TN), lambda bi, i, j: (0, jnp.minimum(j, PROJ_NJ - 1))),
        pl.BlockSpec((PROJ_TM, LANES), lambda bi, i, j: (i, 0)),
        pl.BlockSpec((PROJ_TM, LANES), lambda bi, i, j: (i, 0)),
    ]


def _project_prompt(x, g, w_bf16, cos, sin):
    b, t, _ = x.shape
    jkb, jqb = COL_KB // PROJ_TN, COL_QB // PROJ_TN
    n_res = 3 * WIDTH_B // PROJ_TN
    p_map = lambda bi, i, j: (bi, i, jnp.maximum(j - 1, 0))
    kv_map = lambda bi, i, j: (bi, i, jnp.maximum(j - 1 - jkb, 0))
    res_map = lambda bi, i, j: (bi, 0, i, jnp.clip(j - 1 - jqb, 0, n_res - 1))
    return pl.pallas_call(
        functools.partial(_proj_kernel, prompt=True),
        grid=(b, t // PROJ_TM, PROJ_NJ + 1),
        in_specs=_proj_in_specs(pl.BlockSpec((None, PROJ_TM, D_MODEL), lambda bi, i, j: (bi, i, 0))),
        out_specs=[
            pl.BlockSpec((None, PROJ_TM, PROJ_TN), p_map),
            pl.BlockSpec((None, PROJ_TM, PROJ_TN), kv_map),
            pl.BlockSpec((None, RES, PROJ_TM // RES, PROJ_TN), res_map),
        ],
        out_shape=[
            jax.ShapeDtypeStruct((b, t, PROJ_WIDTH), _bf16),
            jax.ShapeDtypeStruct((b, t, KV_COLS), _f32),
            jax.ShapeDtypeStruct((b, RES, t // RES, 3 * WIDTH_B), _bf16),
        ],
        scratch_shapes=[pltpu.VMEM((PROJ_TM, D_MODEL), _bf16), pltpu.VMEM((PROJ_TM, PROJ_TN), _f32),
                        pltpu.VMEM((PROJ_TN // LANES, PROJ_TM, LANES), _f32)],
        compiler_params=_cparams(("arbitrary", "arbitrary", "arbitrary")),
        name="proj_prompt",
    )(x, g, w_bf16, cos, sin)


def _project_sample(x, g, w_bf16, cos, sin):
    n = x.shape[0]
    return pl.pallas_call(
        functools.partial(_proj_kernel, prompt=False),
        grid=(1, n // PROJ_TM, PROJ_NJ + 1),
        in_specs=_proj_in_specs(pl.BlockSpec((PROJ_TM, D_MODEL), lambda bi, i, j: (i, 0))),
        out_specs=pl.BlockSpec((PROJ_TM, PROJ_TN), lambda bi, i, j: (i, jnp.maximum(j - 1, 0))),
        out_shape=jax.ShapeDtypeStruct((n, PROJ_WIDTH), _f32),
        scratch_shapes=[pltpu.VMEM((PROJ_TM, D_MODEL), _bf16), pltpu.VMEM((PROJ_TM, PROJ_TN), _f32)],
        compiler_params=_cparams(("arbitrary", "arbitrary", "arbitrary")),
        name="proj_sample",
    )(x, g, w_bf16, cos, sin)


def _half_masks():
    lane = _lane_iota((1, LANES))
    return lane < HEAD_DIM


def _dup_half(x, g):
    lo = _half_masks()
    r = pltpu.roll(x, HEAD_DIM, axis=x.ndim - 1)
    return jnp.where(lo, x, r) if g == 0 else jnp.where(lo, r, x)


def _band_attn_kernel(*refs, il, gqa):
    if gqa:
        sink_ref, q_ref, kp_ref, kc_ref, vp_ref, vc_ref, o_ref = refs
    else:
        q_ref, kp_ref, kc_ref, vp_ref, vc_ref, o_ref, lse_ref = refs
    blk = pl.program_id(2)
    q = q_ref[...].reshape(BLOCK, -1)
    k2 = jnp.concatenate([kp_ref[...].reshape(BLOCK, -1), kc_ref[...].reshape(BLOCK, -1)], axis=0)
    v2 = jnp.concatenate([vp_ref[...].reshape(BLOCK, -1), vc_ref[...].reshape(BLOCK, -1)], axis=0)

    per = BLOCK // il
    w = lambda idx: il * (idx % per) + idx // per
    qi = lax.broadcasted_iota(jnp.int32, (2 * BLOCK, 2 * BLOCK), 0) % BLOCK
    kj = lax.broadcasted_iota(jnp.int32, (2 * BLOCK, 2 * BLOCK), 1)
    cur = kj >= BLOCK
    dist = w(qi) - w(kj % BLOCK) + jnp.where(cur, 0, BLOCK)
    mask = (dist >= 0) & (dist <= BLOCK) & (cur | (blk > 0))

    lo = _half_masks()
    zero = jnp.zeros((), q.dtype)
    scale = HEAD_DIM ** -0.5
    if gqa:
        kf, vf = k2.astype(_f32), v2.astype(_f32)
        kdup = [_dup_half(kf, g).astype(_bf16) for g in range(N_KV_A)]
        vdup = [_dup_half(vf, g).astype(_bf16) for g in range(N_KV_A)]
    for hp in range(q.shape[1] // LANES):
        qp = q[:, hp * LANES:(hp + 1) * LANES]
        if gqa:
            kk = kdup[(2 * hp) // GROUP_A]
            vv = vdup[(2 * hp) // GROUP_A]
        else:
            kk = k2[:, hp * LANES:(hp + 1) * LANES]
            vv = v2[:, hp * LANES:(hp + 1) * LANES]
        qs = jnp.concatenate([jnp.where(lo, qp, zero), jnp.where(lo, zero, qp)], axis=0)
        s = lax.dot_general(qs, kk, (((1,), (1,)), ((), ())), preferred_element_type=_f32) * scale
        s = jnp.where(mask, s, NEG_INF)
        m = jnp.max(s, axis=-1, keepdims=True)
        p = jnp.exp(s - m)
        den = jnp.sum(p, axis=-1, keepdims=True)
        o2 = jnp.dot(p.astype(_bf16), vv, preferred_element_type=_f32) / den
        lse = m + jnp.log(den)
        if gqa:
            o2 = jnp.concatenate([
                o2[:BLOCK] * jax.nn.sigmoid(lse[:BLOCK] - sink_ref[2 * hp]),
                o2[BLOCK:] * jax.nn.sigmoid(lse[BLOCK:] - sink_ref[2 * hp + 1])], axis=0)
        o_ref[:, hp * LANES:(hp + 1) * LANES] = jnp.where(lo, o2[:BLOCK], o2[BLOCK:])
        if not gqa:
            lse_ref[:, hp * LANES:(hp + 1) * LANES] = jnp.where(
                lo, jnp.broadcast_to(lse[:BLOCK], (BLOCK, LANES)), jnp.broadcast_to(lse[BLOCK:], (BLOCK, LANES)))


def _band_attention(arr, n_seq, seq_len, row_block, col_blocks, *, il, sinks=None):
    gqa = sinks is not None
    b = arr.shape[0]
    nblk = seq_len // BLOCK
    wq = WIDTH_A if gqa else WIDTH_B
    wk = LANES if gqa else WIDTH_B
    cq, ck, cv = col_blocks
    assert len(arr.shape) == 3 + len(row_block)

    def spec(width, col, prev):
        def imap(bi, si, blk):
            r = jnp.maximum(blk - 1, 0) if prev else blk
            return (bi, si) + ((0, r) if len(row_block) == 2 else (r,)) + (col,)
        return pl.BlockSpec((None, None) + tuple(row_block) + (width,), imap)

    in_specs = [spec(wq, cq, False), spec(wk, ck, True), spec(wk, ck, False), spec(wk, cv, True), spec(wk, cv, False)]
    args = [arr] * 5
    if gqa:
        in_specs = [pl.BlockSpec(memory_space=pltpu.SMEM)] + in_specs
        args = [sinks] + args
    out_map = lambda bi, si, blk: (bi, si, blk, 0)
    n_out = 1 if gqa else 2
    outs = pl.pallas_call(
        functools.partial(_band_attn_kernel, il=il, gqa=gqa),
        grid=(b, n_seq, nblk),
        in_specs=in_specs,
        out_specs=[pl.BlockSpec((None, None, BLOCK, wq), out_map)] * n_out,
        out_shape=[jax.ShapeDtypeStruct((b, n_seq, seq_len, wq), _f32)] * n_out,
        compiler_params=_cparams(("arbitrary", "arbitrary", "arbitrary")),
        name="band_attn_a" if gqa else f"band_attn_b_il{il}_{n_seq}",
    )(*args)
    return outs


def _pad_rows(x, rows):
    return jnp.concatenate([x, jnp.zeros((rows - x.shape[0], x.shape[1]), x.dtype)], axis=0)


def _shift_in(cache, new_t, steps):
    lane = _lane_iota((1, LANES))
    keep = lane < LANES - steps
    ncol = cache.shape[1] // LANES
    cols = []
    prev = pltpu.roll(cache[:, :LANES], LANES - steps, axis=1)
    for c in range(ncol):
        nxt = new_t if c == ncol - 1 else pltpu.roll(cache[:, (c + 1) * LANES:(c + 2) * LANES], LANES - steps, axis=1)
        cols.append(jnp.where(keep, prev, nxt))
        prev = nxt
    return jnp.concatenate(cols, axis=1) if ncol > 1 else cols[0]


def _new_cols(new_rows, bi, s_len):
    t = jnp.transpose(_pad_rows(new_rows, LANES))
    return pltpu.roll(t, (LANES - s_len - bi * s_len) % LANES, axis=1)


def _sample_a_kernel(sink_ref, q_ref, kvn_ref, kc_ref, vc_ref, o_ref, ko_ref, vo_ref, *, s_len):
    nb = 8 // s_len
    q8 = q_ref[...]
    kn8 = kvn_ref[:, :LANES]
    vn8 = kvn_ref[:, LANES:]
    lo = _half_masks()
    row = lax.broadcasted_iota(jnp.int32, (8, 1), 0)
    scale = HEAD_DIM ** -0.5
    npair = GROUP_A // 2
    rows_g = npair * 16
    qrow = lax.broadcasted_iota(jnp.int32, (rows_g, LANES), 0) % 8
    lane = _lane_iota((rows_g, LANES))
    qb, qs_ = qrow // s_len, qrow % s_len
    out = jnp.zeros((8, WIDTH_A), _f32)
    for bi in range(nb):
        mine = (row // s_len) == bi
        kst = kc_ref[bi].reshape(N_KV_A * HEAD_DIM, WINDOW_A)
        vst = vc_ref[bi].reshape(N_KV_A * HEAD_DIM, WINDOW_A)
        ko_ref[bi] = _shift_in(kst, _new_cols(kn8, bi, s_len), s_len).reshape(N_KV_A, HEAD_DIM, WINDOW_A)
        vo_ref[bi] = _shift_in(vst, _new_cols(vn8, bi, s_len), s_len).reshape(N_KV_A, HEAD_DIM, WINDOW_A)
        kpos = PAST_LEN - WINDOW_A + lane
        d_c = PAST_LEN + qs_ - kpos
        mask_c = (d_c >= 0) & (d_c <= WINDOW_A) & (kpos >= 0)
        d_n = qs_ - lane % s_len
        mask_n = (lane < 8) & ((lane // s_len) == bi) & (d_n >= 0) & (d_n <= WINDOW_A)
        cols = []
        for g in range(N_KV_A):
            kdup = jnp.concatenate([kst[g * HEAD_DIM:(g + 1) * HEAD_DIM]] * 2, axis=0).astype(_bf16)
            vdup = jnp.concatenate([vst[g * HEAD_DIM:(g + 1) * HEAD_DIM]] * 2, axis=0).astype(_bf16)
            kn = _pad_rows(_dup_half(kn8, g), LANES).astype(_bf16)
            vn = _pad_rows(_dup_half(vn8, g), LANES).astype(_bf16)
            parts = []
            for pp in range(npair):
                hp = g * npair + pp
                qp = jnp.where(mine, q8[:, hp * LANES:(hp + 1) * LANES], 0.0)
                parts += [jnp.where(lo, qp, 0.0), jnp.where(lo, 0.0, qp)]
            qs = jnp.concatenate(parts, axis=0).astype(_bf16)
            s_c = jnp.dot(qs, kdup, preferred_element_type=_f32) * scale
            s_n = lax.dot_general(qs, kn, (((1,), (1,)), ((), ())), preferred_element_type=_f32) * scale
            s_c = jnp.where(mask_c, s_c, NEG_INF)
            s_n = jnp.where(mask_n, s_n, NEG_INF)
            m = jnp.maximum(jnp.max(s_c, axis=-1, keepdims=True), jnp.max(s_n, axis=-1, keepdims=True))
            p_c, p_n = jnp.exp(s_c - m), jnp.exp(s_n - m)
            den = jnp.sum(p_c, axis=-1, keepdims=True) + jnp.sum(p_n, axis=-1, keepdims=True)
            o = lax.dot_general(p_c.astype(_bf16), vdup, (((1,), (1,)), ((), ())), preferred_element_type=_f32)
            o = (o + jnp.dot(p_n.astype(_bf16), vn, preferred_element_type=_f32)) / den
            lse = m + jnp.log(den)
            for pp in range(npair):
                h0 = 2 * (g * npair + pp)
                r0 = pp * 16
                o0 = o[r0:r0 + 8] * jax.nn.sigmoid(lse[r0:r0 + 8] - sink_ref[h0])
                o1 = o[r0 + 8:r0 + 16] * jax.nn.sigmoid(lse[r0 + 8:r0 + 16] - sink_ref[h0 + 1])
                cols.append(jnp.where(lo, o0, o1))
        out = jnp.where(mine, jnp.concatenate(cols, axis=1), out)
    o_ref[...] = out


def _sample_attention_a(ps, kc_t, vc_t, sinks, s_len):
    n = ps.shape[0]
    nb = 8 // s_len
    bd = n // s_len
    cache_spec = pl.BlockSpec((nb, N_KV_A, HEAD_DIM, WINDOW_A), lambda i: (i, 0, 0, 0))
    return pl.pallas_call(
        functools.partial(_sample_a_kernel, s_len=s_len),
        grid=(n // 8,),
        in_specs=[
            pl.BlockSpec(memory_space=pltpu.SMEM),
            pl.BlockSpec((8, WIDTH_A), lambda i: (i, 0)),
            pl.BlockSpec((8, 2 * KV_WIDTH_A), lambda i: (i, COL_KA // (2 * KV_WIDTH_A))),
            cache_spec, cache_spec,
        ],
        out_specs=[pl.BlockSpec((8, WIDTH_A), lambda i: (i, 0)), cache_spec, cache_spec],
        out_shape=[
            jax.ShapeDtypeStruct((n, WIDTH_A), _f32),
            jax.ShapeDtypeStruct((bd, N_KV_A, HEAD_DIM, WINDOW_A), _f32),
            jax.ShapeDtypeStruct((bd, N_KV_A, HEAD_DIM, WINDOW_A), _f32),
        ],
        compiler_params=_cparams(("arbitrary",)),
        name="sample_attn_a",
    )(sinks, ps, ps, kc_t, vc_t)


def _sample_b_kernel(q_ref, kn_ref, vn_ref, kc_ref, vc_ref, o_ref, ko_ref, vo_ref, *, s_len):
    nb = 8 // s_len
    q8 = q_ref[...]
    kn8 = kn_ref[...]
    vn8 = vn_ref[...]
    lo = _half_masks()
    row = lax.broadcasted_iota(jnp.int32, (8, 1), 0)
    scale = HEAD_DIM ** -0.5
    qrow = lax.broadcasted_iota(jnp.int32, (16, 1), 0) % 8
    qs_ = qrow % s_len
    kn = _pad_rows(kn8, LANES).astype(_bf16)
    vn = _pad_rows(vn8, LANES).astype(_bf16)
    lane_n = _lane_iota((16, LANES))
    out = jnp.zeros((8, LANES), _f32)
    for bi in range(nb):
        mine = (row // s_len) == bi
        kst = kc_ref[bi].reshape(LANES, WINDOW_B)
        vst = vc_ref[bi].reshape(LANES, WINDOW_B)
        ko_ref[bi] = _shift_in(kst, _new_cols(kn8, bi, s_len), s_len).reshape(2, HEAD_DIM, WINDOW_B)
        vo_ref[bi] = _shift_in(vst, _new_cols(vn8, bi, s_len), s_len).reshape(2, HEAD_DIM, WINDOW_B)
        qp = jnp.where(mine, q8, 0.0)
        qs = jnp.concatenate([jnp.where(lo, qp, 0.0), jnp.where(lo, 0.0, qp)], axis=0).astype(_bf16)
        s_c = jnp.dot(qs, kst.astype(_bf16), preferred_element_type=_f32) * scale
        s_n = lax.dot_general(qs, kn, (((1,), (1,)), ((), ())), preferred_element_type=_f32) * scale
        d_n = qs_ - lane_n % s_len
        ok_n = (lane_n < 8) & ((lane_n // s_len) == bi) & (d_n >= 0)
        branches = []
        for w, r in DILATED_PAIRS:
            lo_i = WINDOW_B - w
            lane_c = lo_i + _lane_iota((16, w))
            d_c = WINDOW_B + qs_ - lane_c
            ok_c = (d_c % r == 0) & (d_c <= w) & (PAST_LEN - WINDOW_B + lane_c >= 0)
            sc = jnp.where(ok_c, s_c[:, lo_i:], NEG_INF)
            sn = jnp.where(ok_n & (d_n % r == 0) & (d_n <= w), s_n, NEG_INF)
            m = jnp.maximum(jnp.max(sc, axis=-1, keepdims=True), jnp.max(sn, axis=-1, keepdims=True))
            branches.append((lo_i, sc, sn, m))
        m_all = functools.reduce(jnp.maximum, [br[3] for br in branches])
        p_c = jnp.zeros((16, WINDOW_B), _f32)
        p_n = jnp.zeros((16, LANES), _f32)
        for lo_i, sc, sn, m in branches:
            e = jnp.exp(sc - m_all)
            p_c = p_c + (jnp.concatenate([jnp.zeros((16, lo_i), _f32), e], axis=1) if lo_i else e)
            p_n = p_n + jnp.exp(sn - m_all)
        den = jnp.sum(p_c, axis=-1, keepdims=True) + jnp.sum(p_n, axis=-1, keepdims=True)
        o = lax.dot_general(p_c.astype(_bf16), vst.astype(_bf16), (((1,), (1,)), ((), ())), preferred_element_type=_f32)
        o = (o + jnp.dot(p_n.astype(_bf16), vn, preferred_element_type=_f32)) / den
        out = jnp.where(mine, jnp.where(lo, o[:8], o[8:]), out)
    o_ref[...] = out


def _sample_attention_b(ps, kc_t, vc_t, s_len):
    n = ps.shape[0]
    nb = 8 // s_len
    bd = n // s_len
    npair = N_HEADS_B // 2
    cache_spec = pl.BlockSpec((nb, 2, HEAD_DIM, WINDOW_B), lambda i, hp: (i, hp, 0, 0))
    col = lambda base: (lambda i, hp: (i, base // LANES + hp))
    return pl.pallas_call(
        functools.partial(_sample_b_kernel, s_len=s_len),
        grid=(n // 8, npair),
        in_specs=[
            pl.BlockSpec((8, LANES), col(COL_QB)),
            pl.BlockSpec((8, LANES), col(COL_KB)),
            pl.BlockSpec((8, LANES), col(COL_VB)),
            cache_spec, cache_spec,
        ],
        out_specs=[pl.BlockSpec((8, LANES), lambda i, hp: (i, hp)), cache_spec, cache_spec],
        out_shape=[
            jax.ShapeDtypeStruct((n, WIDTH_B), _f32),
            jax.ShapeDtypeStruct((bd, N_HEADS_B, HEAD_DIM, WINDOW_B), _f32),
            jax.ShapeDtypeStruct((bd, N_HEADS_B, HEAD_DIM, WINDOW_B), _f32),
        ],
        compiler_params=_cparams(("arbitrary", "arbitrary")),
        name="sample_attn_b",
    )(ps, ps, ps, kc_t, vc_t)


def _rms(x, g):
    return x * lax.rsqrt(jnp.mean(x * x, axis=-1, keepdims=True) + RMS_EPS) * g


def _route(z):
    lane = _lane_iota(z.shape)
    big = jnp.int32(1 << 20)
    glane = lane < N_GROUPS
    gmax = jnp.max(jnp.where(glane, z, -jnp.inf), axis=-1, keepdims=True)
    grp = jnp.min(jnp.where(glane & (z == gmax), lane, big), axis=-1, keepdims=True)
    p_grp = 1.0 / jnp.sum(jnp.where(glane, jnp.exp(z - gmax), 0.0), axis=-1, keepdims=True)
    elane = (lane >= N_GROUPS) & (lane < N_GROUPS + N_EXPERTS) & ((lane - N_GROUPS) // EXPERTS_PER_GROUP == grp)
    t1 = jnp.max(jnp.where(elane, z, -jnp.inf), axis=-1, keepdims=True)
    i1 = jnp.min(jnp.where(elane & (z == t1), lane, big), axis=-1, keepdims=True)
    elane2 = elane & (lane != i1)
    t2 = jnp.max(jnp.where(elane2, z, -jnp.inf), axis=-1, keepdims=True)
    i2 = jnp.min(jnp.where(elane2 & (z == t2), lane, big), axis=-1, keepdims=True)
    e = jnp.exp(t2 - t1)
    w1 = p_grp * (1.0 / (1.0 + e))
    w2 = p_grp * (e / (1.0 + e))
    experts = jnp.where(lane == 0, i1 - N_GROUPS, jnp.where(lane == 1, i2 - N_GROUPS, 0))
    weights = jnp.where(lane == 0, w1, jnp.where(lane == 1, w2, 0.0))
    return experts, weights


def _store_row_tiles(ref, val):
    t, c = val.shape
    per = c // LANES
    for j in range(per):
        ref[pl.ds(j, t, stride=per), :] = val[:, j * LANES:(j + 1) * LANES]


def _load_row_tiles(ref, t):
    per = ref.shape[0] // t
    return [ref[pl.ds(j, t, stride=per), :] for j in range(per)]


def _merge_kernel(*refs, n_branch):
    n_lse = n_branch if n_branch > 1 else 0
    oa_ref = refs[0]
    ob_refs = refs[1:1 + n_branch]
    lse_refs = refs[1 + n_branch:1 + n_branch + n_lse]
    x_ref, ga_ref, gb_ref, wo_ref, gf_ref, wr_ref, br_ref = refs[1 + n_branch + n_lse:8 + n_branch + n_lse]
    x1_ref, hp_ref, re_ref, rw_ref = refs[-4:]
    if n_branch == 1:
        ob = ob_refs[0][...]
    else:
        lses = [r[...] for r in lse_refs]
        lmax = functools.reduce(jnp.maximum, lses)
        es = [jnp.exp(l - lmax) for l in lses]
        tot = functools.reduce(jnp.add, es)
        ob = functools.reduce(jnp.add, [(e / tot) * r[...] for e, r in zip(es, ob_refs)])
    o = jnp.concatenate([_rms(oa_ref[...], ga_ref[...]), _rms(ob, gb_ref[...])], axis=-1).astype(_bf16)
    x1 = x_ref[...] + jnp.dot(o, wo_ref[...], preferred_element_type=_f32)
    x1_ref[...] = x1
    h2 = _rms(x1, gf_ref[...])
    h2b = h2.astype(_bf16)
    z = jnp.dot(h2b, wr_ref[...], preferred_element_type=_f32) + br_ref[...]
    experts, weights = _route(z)
    re_ref[...] = experts
    rw_ref[...] = weights
    bits = pltpu.bitcast(h2b.astype(_f32), jnp.uint32)
    half = D_MODEL // 2
    _store_row_tiles(hp_ref, bits[:, :half] | (bits[:, half:] >> 16))


PACK_TILES = D_MODEL // 2 // LANES
ROW_TILES = D_MODEL // LANES


def _merge(oa, obs, lses, x, ga, gb, wo_bf16, gf, wr_bf16, br):
    n = oa.shape[0]
    nb = len(obs)
    assert len(lses) == (nb if nb > 1 else 0)
    row = lambda r, w: pl.BlockSpec((r, w), lambda i: (i, 0))
    full = lambda a: pl.BlockSpec(a.shape, lambda i: (0,) * a.ndim)
    consts = [ga, gb, wo_bf16, gf, wr_bf16, br]
    return pl.pallas_call(
        functools.partial(_merge_kernel, n_branch=nb),
        grid=(n // ROW_TM,),
        in_specs=([row(ROW_TM, WIDTH_A)] + [row(ROW_TM, WIDTH_B)] * (nb + len(lses)) + [row(ROW_TM, D_MODEL)]
                  + [full(a) for a in consts]),
        out_specs=[row(ROW_TM, D_MODEL), row(ROW_TM * PACK_TILES, LANES), row(ROW_TM, LANES), row(ROW_TM, LANES)],
        out_shape=[jax.ShapeDtypeStruct((n, D_MODEL), _f32),
                   jax.ShapeDtypeStruct((n * PACK_TILES, LANES), jnp.uint32),
                   jax.ShapeDtypeStruct((n, LANES), jnp.int32),
                   jax.ShapeDtypeStruct((n, LANES), _f32)],
        compiler_params=_cparams(("arbitrary",)),
        name=f"merge_{nb}",
    )(oa, *obs, *lses, x, *consts)


def _rank_kernel(re_ref, rank_ref, cnt_ref, carry):
    @pl.when(pl.program_id(0) == 0)
    def _():
        carry[...] = jnp.zeros_like(carry)

    e = re_ref[...]
    lane = _lane_iota(e.shape)
    r_i = lax.broadcasted_iota(jnp.int32, (ROW_TM, ROW_TM), 0)
    c_i = lax.broadcasted_iota(jnp.int32, (ROW_TM, ROW_TM), 1)
    before = (c_i < r_i).astype(_bf16)
    base = carry[...]
    ranks = []
    for k in range(2):
        onehot = lane == e[:, k:k + 1]
        oh = onehot.astype(_f32)
        earlier = jnp.dot(before, onehot.astype(_bf16), preferred_element_type=_f32)
        ranks.append(jnp.sum(oh * (earlier + base), axis=-1, keepdims=True))
        base = base + jnp.sum(oh, axis=0, keepdims=True)
    carry[...] = base
    cnt_ref[...] = base
    rank_ref[...] = jnp.where(lane == 0, ranks[0], jnp.where(lane == 1, ranks[1], 0.0)).astype(jnp.int32)


def _rank(route_e):
    n = route_e.shape[0]
    return pl.pallas_call(
        _rank_kernel,
        grid=(n // ROW_TM,),
        in_specs=[pl.BlockSpec((ROW_TM, LANES), lambda i: (i, 0))],
        out_specs=[pl.BlockSpec((ROW_TM, LANES), lambda i: (i, 0)), pl.BlockSpec((1, LANES), lambda i: (0, 0))],
        out_shape=[jax.ShapeDtypeStruct((n, LANES), jnp.int32), jax.ShapeDtypeStruct((1, LANES), _f32)],
        scratch_shapes=[pltpu.VMEM((1, LANES), _f32)],
        compiler_params=_cparams(("arbitrary",)),
        name="moe_rank",
    )(route_e)


PAIRS_PER_TILE = 2 * ROW_TM


def _row_copy(src, dst, i_src, i_dst, per, sem):
    return pltpu.make_async_copy(src.at[pl.ds(pl.multiple_of(i_src * per, per), per), :],
                                 dst.at[pl.ds(pl.multiple_of(i_dst * per, per), per), :], sem)


def _dispatch_kernel(dest_ref, src_ref, _init_ref, dst_ref, sem):
    copy = lambda p: _row_copy(src_ref, dst_ref, p // 2, dest_ref[p], PACK_TILES, sem)

    def issue(p, c):
        copy(p).start()
        return c

    def drain(p, c):
        copy(p).wait()
        return c

    lax.fori_loop(0, PAIRS_PER_TILE, issue, 0)
    lax.fori_loop(0, PAIRS_PER_TILE, drain, 0)


def _dispatch(dest_flat, src, out):
    anyspec = pl.BlockSpec(memory_space=pl.ANY)
    return pl.pallas_call(
        _dispatch_kernel,
        grid=(dest_flat.shape[0] // PAIRS_PER_TILE,),
        in_specs=[pl.BlockSpec((PAIRS_PER_TILE,), lambda i: (i,), memory_space=pltpu.SMEM),
                  pl.BlockSpec((ROW_TM * PACK_TILES, LANES), lambda i: (i, 0)), anyspec],
        out_specs=anyspec,
        out_shape=jax.ShapeDtypeStruct(out.shape, out.dtype),
        input_output_aliases={2: 0},
        scratch_shapes=[pltpu.SemaphoreType.DMA],
        compiler_params=pltpu.CompilerParams(dimension_semantics=("arbitrary",), has_side_effects=True,
                                             vmem_limit_bytes=VMEM_LIMIT),
        name="moe_dispatch",
    )(dest_flat, src, out)


def _ffn_kernel(te_ref, nt_ref, x_ref, wg_ref, wu_ref, wd_ref, y_ref):
    i = pl.program_id(0)

    @pl.when(i < nt_ref[0])
    def _():
        packed = _load_row_tiles(x_ref, MOE_TM)
        hi = [pltpu.bitcast(p & jnp.uint32(0xFFFF0000), _f32) for p in packed]
        lo = [pltpu.bitcast(p << 16, _f32) for p in packed]
        x = jnp.concatenate(hi + lo, axis=1).astype(_bf16)
        g = jnp.dot(x, wg_ref[...], preferred_element_type=_f32)
        u = jnp.dot(x, wu_ref[...], preferred_element_type=_f32)
        h = (jax.nn.silu(g) * u).astype(_bf16)
        _store_row_tiles(y_ref, jnp.dot(h, wd_ref[...], preferred_element_type=_f32))

    @pl.when(i >= nt_ref[0])
    def _():
        y_ref[...] = jnp.zeros_like(y_ref)


def _expert_ffn(tile_expert, n_tiles_used, x_disp, wg, wu, wd):
    rows = x_disp.shape[0] // PACK_TILES
    return pl.pallas_call(
        _ffn_kernel,
        grid_spec=pltpu.PrefetchScalarGridSpec(
            num_scalar_prefetch=2,
            grid=(rows // MOE_TM,),
            in_specs=[
                pl.BlockSpec((MOE_TM * PACK_TILES, LANES), lambda i, te, nt: (i, 0)),
                pl.BlockSpec((None, D_MODEL, D_EXPERT), lambda i, te, nt: (te[i], 0, 0)),
                pl.BlockSpec((None, D_MODEL, D_EXPERT), lambda i, te, nt: (te[i], 0, 0)),
                pl.BlockSpec((None, D_EXPERT, D_MODEL), lambda i, te, nt: (te[i], 0, 0)),
            ],
            out_specs=pl.BlockSpec((MOE_TM * ROW_TILES, LANES), lambda i, te, nt: (i, 0)),
        ),
        out_shape=jax.ShapeDtypeStruct((rows * ROW_TILES, LANES), _f32),
        compiler_params=_cparams(("arbitrary",)),
        name="moe_ffn",
    )(tile_expert, n_tiles_used, x_disp, wg, wu, wd)


def _final_kernel(dcur_ref, dnext_ref, x1_ref, rw_ref, g_ref, y_hbm, o_ref, ybuf, sems, *, n_tiles):
    i = pl.program_id(0)
    slot = i % 2

    def copy(dref, p, s):
        return _row_copy(y_hbm, ybuf.at[s, p % 2], dref[p], p // 2, ROW_TILES, sems.at[s])

    def issue(dref, s):
        def body(p, c):
            copy(dref, p, s).start()
            return c
        lax.fori_loop(0, PAIRS_PER_TILE, body, 0)

    @pl.when(i == 0)
    def _():
        issue(dcur_ref, 0)

    @pl.when(i + 1 < n_tiles)
    def _():
        issue(dnext_ref, 1 - slot)

    def drain(p, c):
        copy(dcur_ref, p, slot).wait()
        return c

    lax.fori_loop(0, PAIRS_PER_TILE, drain, 0)
    rw = rw_ref[...]
    y0 = jnp.concatenate(_load_row_tiles(ybuf.at[slot, 0], ROW_TM), axis=1)
    y1 = jnp.concatenate(_load_row_tiles(ybuf.at[slot, 1], ROW_TM), axis=1)
    moe = y0 * rw[:, 0:1] + y1 * rw[:, 1:2]
    o_ref[...] = _rms(x1_ref[...] + moe, g_ref[...])


def _final(dest_flat, x1, rw, g, y_disp):
    n = x1.shape[0]
    nt = n // ROW_TM
    dspec = lambda f: pl.BlockSpec((PAIRS_PER_TILE,), f, memory_space=pltpu.SMEM)
    return pl.pallas_call(
        functools.partial(_final_kernel, n_tiles=nt),
        grid=(nt,),
        in_specs=[
            dspec(lambda i: (i,)),
            dspec(lambda i: (jnp.minimum(i + 1, nt - 1),)),
            pl.BlockSpec((ROW_TM, D_MODEL), lambda i: (i, 0)),
            pl.BlockSpec((ROW_TM, LANES), lambda i: (i, 0)),
            pl.BlockSpec((1, D_MODEL), lambda i: (0, 0)),
            pl.BlockSpec(memory_space=pl.ANY),
        ],
        out_specs=pl.BlockSpec((ROW_TM, D_MODEL), lambda i: (i, 0)),
        out_shape=jax.ShapeDtypeStruct((n, D_MODEL), _f32),
        scratch_shapes=[pltpu.VMEM((2, 2, ROW_TM * ROW_TILES, LANES), _f32), pltpu.SemaphoreType.DMA((2,))],
        compiler_params=_cparams(("arbitrary",)),
        name="moe_combine_final",
    )(dest_flat, dest_flat, x1, rw, g, y_disp)


def _moe_plan(route_e, rank, counts):
    counts = counts[0, :N_EXPERTS].astype(jnp.int32)
    padded = (counts + MOE_TM - 1) // MOE_TM * MOE_TM
    pend = jnp.cumsum(padded)
    pstart = pend - padded
    e = route_e[:, :2]
    dest = (pstart[e] + rank[:, :2]).reshape(-1)
    n_pairs = dest.shape[0]
    n_tiles = (n_pairs + N_EXPERTS * (MOE_TM - 1) + MOE_TM - 1) // MOE_TM
    tile_start = jnp.arange(n_tiles, dtype=jnp.int32) * MOE_TM
    tile_expert = jnp.minimum(jnp.sum(pend[None, :] <= tile_start[:, None], axis=1), N_EXPERTS - 1).astype(jnp.int32)
    n_used = (pend[-1] // MOE_TM).astype(jnp.int32).reshape(1)
    return dest, tile_expert, n_used, n_tiles


def kernel(x_prompt, x_sample, cache_a_k, cache_a_v, cache_b_k, cache_b_v, g_attn, w_in, attn_sinks, g_out_a,
           g_out_b, w_out, g_ffn, w_router_group, b_router_group, w_router_expert, b_router_expert, w_gate, w_up,
           w_down, g_final):
    depth = g_attn.shape[0]
    assert depth == 1
    b, t, _ = x_prompt.shape
    bd, s_len, _ = x_sample.shape
    assert t % (RES * BLOCK) == 0 and t % PROJ_TM == 0 and 8 % s_len == 0
    n_p, n_s = b * t, bd * s_len
    n_tok = n_p + n_s
    l = 0

    w_l = w_in[l]
    w_in_b = jnp.concatenate([w_l[:, :WIDTH_A], w_l[:, WIDTH_A + 2 * KV_WIDTH_A:],
                              w_l[:, WIDTH_A:WIDTH_A + 2 * KV_WIDTH_A]], axis=1).astype(_bf16)
    w_out_b = w_out[l].astype(_bf16)
    wg_b, wu_b, wd_b = w_gate[l].astype(_bf16), w_up[l].astype(_bf16), w_down[l].astype(_bf16)
    pad = LANES - N_GROUPS - N_EXPERTS
    w_r = jnp.concatenate([w_router_group[l], w_router_expert[l], jnp.zeros((D_MODEL, pad), _f32)], axis=1)
    b_r = jnp.concatenate([b_router_group[l], b_router_expert[l], jnp.zeros((pad,), _f32)])[None]
    g_a, g_oa, g_ob, g_f = g_attn[l][None], g_out_a[l][None], g_out_b[l][None], g_ffn[l][None]
    sinks = attn_sinks[l]

    cos_p, sin_p = _rope_tables(jnp.arange(t))
    p_nat, kv_f32, p_res = _project_prompt(x_prompt, g_a, w_in_b, cos_p, sin_p)
    (oa,) = _band_attention(p_nat[:, None], 1, t, (BLOCK,), (0, COL_KA // LANES, COL_VA // LANES), il=1, sinks=sinks)
    cb = (COL_QB // WIDTH_B, COL_KB // WIDTH_B, COL_VB // WIDTH_B)
    o1, l1 = _band_attention(p_nat[:, None], 1, t, (BLOCK,), cb, il=1)
    o4, l4 = _band_attention(p_res.reshape(b, 4, 4, t // RES, 3 * WIDTH_B), 4, t // 4, (4, BLOCK // 4), (0, 1, 2), il=4)
    o16, l16 = _band_attention(p_res, RES, t // RES, (BLOCK,), (0, 1, 2), il=1)

    def from_res4(a):
        a = a.reshape(b, 4, t // RES // (BLOCK // 4), 4, BLOCK // 4, WIDTH_B)
        return a.transpose(0, 2, 4, 3, 1, 5).reshape(n_p, WIDTH_B)

    def from_res16(a):
        a = a.reshape(b, 4, 4, t // RES, WIDTH_B)
        return a.transpose(0, 3, 2, 1, 4).reshape(n_p, WIDTH_B)

    obs_p = [o1.reshape(n_p, WIDTH_B), from_res4(o4), from_res16(o16)]
    lses_p = [l1.reshape(n_p, WIDTH_B), from_res4(l4), from_res16(l16)]

    cos_s, sin_s = _rope_tables(PAST_LEN + jnp.arange(n_s) % s_len)
    ps = _project_sample(x_sample.reshape(n_s, D_MODEL), g_a, w_in_b, cos_s, sin_s)
    to_t = lambda c: jnp.transpose(c[l], (0, 2, 3, 1))
    from_t = lambda c: jnp.transpose(c, (0, 3, 1, 2))[None]
    oa_s, ak_s, av_s = _sample_attention_a(ps, to_t(cache_a_k), to_t(cache_a_v), sinks, s_len)
    ob_s, bk_s, bv_s = _sample_attention_b(ps, to_t(cache_b_k), to_t(cache_b_v), s_len)

    consts = (g_oa, g_ob, w_out_b, g_f, w_r.astype(_bf16), b_r)
    x1_p, h2_p, re_p, rw_p = _merge(oa.reshape(n_p, WIDTH_A), obs_p, lses_p, x_prompt.reshape(n_p, D_MODEL), *consts)
    x1_s, h2_s, re_s, rw_s = _merge(oa_s, [ob_s], [], x_sample.reshape(n_s, D_MODEL), *consts)

    route_e = jnp.concatenate([re_p, re_s], axis=0)
    rank, counts = _rank(route_e)
    dest, tile_expert, n_used, n_tiles = _moe_plan(route_e, rank, counts)
    dest_p, dest_s = dest[:2 * n_p], dest[2 * n_p:]
    x_disp = jnp.zeros((n_tiles * MOE_TM * PACK_TILES, LANES), jnp.uint32)
    x_disp = _dispatch(dest_p, h2_p, x_disp)
    x_disp = _dispatch(dest_s, h2_s, x_disp)
    y_disp = _expert_ffn(tile_expert, n_used, x_disp, wg_b, wu_b, wd_b)
    g_fin = g_final[None]
    y_prompt = _final(dest_p, x1_p, rw_p, g_fin, y_disp).reshape(b, t, D_MODEL)
    y_sample = _final(dest_s, x1_s, rw_s, g_fin, y_disp).reshape(bd, s_len, D_MODEL)

    kv = kv_f32
    heads = lambda a, h: a.reshape(b, -1, h, HEAD_DIM)[None]
    bk_p = heads(kv[:, t - WINDOW_B:, 0:WIDTH_B], N_HEADS_B)
    bv_p = heads(kv[:, t - WINDOW_B:, WIDTH_B:2 * WIDTH_B], N_HEADS_B)
    ak_p = heads(kv[:, t - WINDOW_A:, 2 * WIDTH_B:2 * WIDTH_B + KV_WIDTH_A], N_KV_A)
    av_p = heads(kv[:, t - WINDOW_A:, 2 * WIDTH_B + KV_WIDTH_A:], N_KV_A)
    return (y_prompt, y_sample, ak_p, av_p, bk_p, bv_p, from_t(ak_s), from_t(av_s), from_t(bk_s), from_t(bv_s))
```

```python
import functools

import jax
import jax.numpy as jnp
from jax import lax
from jax.experimental import pallas as pl
from jax.experimental.pallas import tpu as pltpu

D_MODEL = 2048
HEAD_DIM = 64
N_HEADS_A = 16
N_KV_A = 2
GROUP_A = N_HEADS_A // N_KV_A
N_HEADS_B = 16
WIDTH_A = N_HEADS_A * HEAD_DIM
WIDTH_B = N_HEADS_B * HEAD_DIM
KV_WIDTH_A = N_KV_A * HEAD_DIM
PROJ_WIDTH = WIDTH_A + 2 * KV_WIDTH_A + 3 * WIDTH_B
WINDOW_A = 128
DILATED_PAIRS = ((128, 1), (512, 4), (2048, 16))
WINDOW_B = 2048
BLOCK = 128
ROPE_THETA = 10000.0
N_GROUPS = 4
EXPERTS_PER_GROUP = 8
N_EXPERTS = N_GROUPS * EXPERTS_PER_GROUP
D_EXPERT = D_MODEL // 2
RMS_EPS = 1e-6
NEG_INF = -1e30
PAST_LEN = 16384

LANES = 128
VMEM_LIMIT = 56 * 1024 * 1024

COL_QA = 0
COL_QB = WIDTH_A
COL_KB = COL_QB + WIDTH_B
COL_VB = COL_KB + WIDTH_B
COL_KA = COL_VB + WIDTH_B
COL_VA = COL_KA + KV_WIDTH_A
KV_COLS = 2 * WIDTH_B + 2 * KV_WIDTH_A

PROJ_TM = 512
PROJ_TN = 256
RES = 16
ROW_TM = 256
MOE_TM = 256

_f32 = jnp.float32
_bf16 = jnp.bfloat16


def _cparams(sem):
    return pltpu.CompilerParams(dimension_semantics=sem, vmem_limit_bytes=VMEM_LIMIT)


def _lane_iota(shape):
    return lax.broadcasted_iota(jnp.int32, shape, len(shape) - 1)


def _rope_tables(pos):
    half = HEAD_DIM // 2
    inv_freq = ROPE_THETA ** (-jnp.arange(half, dtype=_f32) / half)
    ang = pos.astype(_f32)[:, None] * inv_freq[None, :]
    cos, sin = jnp.cos(ang), jnp.sin(ang)
    return jnp.tile(cos, (1, 4)), jnp.tile(jnp.concatenate([-sin, sin], axis=-1), (1, 2))


def _proj_kernel(x_ref, g_ref, w_ref, cos_ref, sin_ref, *refs, prompt):
    if prompt:
        p_ref, kv_ref, res_ref, h_scr, mm_scr, rope_scr = refs
    else:
        p_ref, h_scr, mm_scr = refs
    j = pl.program_id(2)
    jj = j - 1

    @pl.when(j == 0)
    def _():
        x = x_ref[...]
        ms = jnp.mean(x * x, axis=-1, keepdims=True)
        h_scr[...] = (x * lax.rsqrt(ms + RMS_EPS) * g_ref[...]).astype(_bf16)
        mm_scr[...] = jnp.zeros_like(mm_scr)

    lane = _lane_iota((1, LANES))
    first_half = (lane % HEAD_DIM) < (HEAD_DIM // 2)
    cos, sin = cos_ref[...], sin_ref[...]
    for half in range(PROJ_TN // LANES):
        cs = slice(half * LANES, (half + 1) * LANES)
        sb = jj * (PROJ_TN // LANES) + half
        is_v = ((sb >= COL_VB // LANES) & (sb < COL_KA // LANES)) | (sb >= COL_VA // LANES)
        a = mm_scr[:, cs]
        swapped = jnp.where(first_half, pltpu.roll(a, LANES - HEAD_DIM // 2, axis=1),
                            pltpu.roll(a, HEAD_DIM // 2, axis=1))
        y = a * jnp.where(is_v, 1.0, cos) + swapped * jnp.where(is_v, 0.0, sin)
        if prompt:
            p_ref[:, cs] = y.astype(_bf16)
            kv_ref[:, cs] = y
            rope_scr[half] = y
        else:
            p_ref[:, cs] = y
    mm_scr[...] = jnp.dot(h_scr[...], w_ref[...], preferred_element_type=_f32)
    if not prompt:
        return

    @pl.when((jj >= COL_QB // PROJ_TN) & (jj < COL_KA // PROJ_TN))
    def _():
        rows = PROJ_TM // RES
        for c in range(RES):
            c16 = 4 * (c % 4) + c // 4
            for half in range(PROJ_TN // LANES):
                res_ref[c, :, half * LANES:(half + 1) * LANES] = (
                    rope_scr[half, pl.ds(c16, rows, stride=RES), :].astype(_bf16))


PROJ_NJ = PROJ_WIDTH // PROJ_TN


def _proj_in_specs(x_spec):
    return [
        x_spec,
        pl.BlockSpec((1, D_MODEL), lambda bi, i, j: (0, 0)),
        pl.BlockSpec((D_MODEL, PROJ_TN), lambda bi, i, j: (0, jnp.minimum(j, PROJ_NJ - 1))),
        pl.BlockSpec((PROJ_TM, LANES), lambda bi, i, j: (i, 0)),
        pl.BlockSpec((PROJ_TM, LANES), lambda bi, i, j: (i, 0)),
    ]


def _project_prompt(x, g, w_bf16, cos, sin):
    b, t, _ = x.shape
    jkb, jqb = COL_KB // PROJ_TN, COL_QB // PROJ_TN
    n_res = 3 * WIDTH_B // PROJ_TN
    p_map = lambda bi, i, j: (bi, i, jnp.maximum(j - 1, 0))
    kv_map = lambda bi, i, j: (bi, i, jnp.maximum(j - 1 - jkb, 0))
    res_map = lambda bi, i, j: (bi, 0, i, jnp.clip(j - 1 - jqb, 0, n_res - 1))
    return pl.pallas_call(
        functools.partial(_proj_kernel, prompt=True),
        grid=(b, t // PROJ_TM, PROJ_NJ + 1),
        in_specs=_proj_in_specs(pl.BlockSpec((None, PROJ_TM, D_MODEL), lambda bi, i, j: (bi, i, 0))),
        out_specs=[
            pl.BlockSpec((None, PROJ_TM, PROJ_TN), p_map),
            pl.BlockSpec((None, PROJ_TM, PROJ_TN), kv_map),
            pl.BlockSpec((None, RES, PROJ_TM // RES, PROJ_TN), res_map),
        ],
        out_shape=[
            jax.ShapeDtypeStruct((b, t, PROJ_WIDTH), _bf16),
            jax.ShapeDtypeStruct((b, t, KV_COLS), _f32),
            jax.ShapeDtypeStruct((b, RES, t // RES, 3 * WIDTH_B), _bf16),
        ],
        scratch_shapes=[pltpu.VMEM((PROJ_TM, D_MODEL), _bf16), pltpu.VMEM((PROJ_TM, PROJ_TN), _f32),
                        pltpu.VMEM((PROJ_TN // LANES, PROJ_TM, LANES), _f32)],
        compiler_params=_cparams(("arbitrary", "arbitrary", "arbitrary")),
        name="proj_prompt",
    )(x, g, w_bf16, cos, sin)


def _project_sample(x, g, w_bf16, cos, sin):
    n = x.shape[0]
    return pl.pallas_call(
        functools.partial(_proj_kernel, prompt=False),
        grid=(1, n // PROJ_TM, PROJ_NJ + 1),
        in_specs=_proj_in_specs(pl.BlockSpec((PROJ_TM, D_MODEL), lambda bi, i, j: (i, 0))),
        out_specs=pl.BlockSpec((PROJ_TM, PROJ_TN), lambda bi, i, j: (i, jnp.maximum(j - 1, 0))),
        out_shape=jax.ShapeDtypeStruct((n, PROJ_WIDTH), _f32),
        scratch_shapes=[pltpu.VMEM((PROJ_TM, D_MODEL), _bf16), pltpu.VMEM((PROJ_TM, PROJ_TN), _f32)],
        compiler_params=_cparams(("arbitrary", "arbitrary", "arbitrary")),
        name="proj_sample",
    )(x, g, w_bf16, cos, sin)


def _half_masks():
    lane = _lane_iota((1, LANES))
    return lane < HEAD_DIM


def _dup_half(x, g):
    lo = _half_masks()
    r = pltpu.roll(x, HEAD_DIM, axis=x.ndim - 1)
    return jnp.where(lo, x, r) if g == 0 else jnp.where(lo, r, x)


def _band_attn_kernel(*refs, il, gqa):
    if gqa:
        sink_ref, q_ref, kp_ref, kc_ref, vp_ref, vc_ref, o_ref = refs
    else:
        q_ref, kp_ref, kc_ref, vp_ref, vc_ref, o_ref, lse_ref = refs
    blk = pl.program_id(2)
    q = q_ref[...].reshape(BLOCK, -1)
    k2 = jnp.concatenate([kp_ref[...].reshape(BLOCK, -1), kc_ref[...].reshape(BLOCK, -1)], axis=0)
    v2 = jnp.concatenate([vp_ref[...].reshape(BLOCK, -1), vc_ref[...].reshape(BLOCK, -1)], axis=0)

    per = BLOCK // il
    w = lambda idx: il * (idx % per) + idx // per
    wi = w(lax.broadcasted_iota(jnp.int32, (2 * BLOCK, BLOCK), 0) % BLOCK)
    wj = w(lax.broadcasted_iota(jnp.int32, (2 * BLOCK, BLOCK), 1))
    newer = wj > wi
    same = wj == wi
    prev_bias = jnp.where(blk > 0, 0.0, NEG_INF)

    lo = _half_masks()
    zero = jnp.zeros((), q.dtype)
    q = q * jnp.asarray(HEAD_DIM ** -0.5, q.dtype)
    if gqa:
        kf, vf = k2.astype(_f32), v2.astype(_f32)
        kdup = [_dup_half(kf, g).astype(_bf16) for g in range(N_KV_A)]
        vdup = [_dup_half(vf, g).astype(_bf16) for g in range(N_KV_A)]
    for hp in range(q.shape[1] // LANES):
        qp = q[:, hp * LANES:(hp + 1) * LANES]
        if gqa:
            kk = kdup[(2 * hp) // GROUP_A]
            vv = vdup[(2 * hp) // GROUP_A]
        else:
            kk = k2[:, hp * LANES:(hp + 1) * LANES]
            vv = v2[:, hp * LANES:(hp + 1) * LANES]
        qs = jnp.concatenate([jnp.where(lo, qp, zero), jnp.where(lo, zero, qp)], axis=0)
        s = lax.dot_general(qs, kk, (((1,), (1,)), ((), ())), preferred_element_type=_f32)
        s_prev = s[:, :BLOCK] + prev_bias
        s_band = jnp.where(newer, s_prev, s[:, BLOCK:])
        s_edge = jnp.sum(jnp.where(same, s_prev, 0.0), axis=-1, keepdims=True)
        m = jnp.maximum(jnp.max(s_band, axis=-1, keepdims=True), s_edge)
        e = jnp.exp(s_band - m)
        e_edge = jnp.exp(s_edge - m)
        den = jnp.sum(e, axis=-1, keepdims=True) + e_edge
        p_prev = jnp.where(newer, e, jnp.where(same, e_edge, 0.0))
        p_cur = jnp.where(newer, 0.0, e)
        p = jnp.concatenate([p_prev, p_cur], axis=1).astype(_bf16)
        o2 = jnp.dot(p, vv, preferred_element_type=_f32) / den
        lse = m + jnp.log(den)
        if gqa:
            o2 = jnp.concatenate([
                o2[:BLOCK] * jax.nn.sigmoid(lse[:BLOCK] - sink_ref[2 * hp]),
                o2[BLOCK:] * jax.nn.sigmoid(lse[BLOCK:] - sink_ref[2 * hp + 1])], axis=0)
        o_ref[:, hp * LANES:(hp + 1) * LANES] = jnp.where(lo, o2[:BLOCK], o2[BLOCK:])
        if not gqa:
            lse_ref[:, hp * LANES:(hp + 1) * LANES] = jnp.where(
                lo, jnp.broadcast_to(lse[:BLOCK], (BLOCK, LANES)), jnp.broadcast_to(lse[BLOCK:], (BLOCK, LANES)))


def _band_attention(arr, n_seq, seq_len, row_block, col_blocks, *, il, sinks=None):
    gqa = sinks is not None
    b = arr.shape[0]
    nblk = seq_len // BLOCK
    wq = WIDTH_A if gqa else WIDTH_B
    wk = LANES if gqa else WIDTH_B
    cq, ck, cv = col_blocks
    assert len(arr.shape) == 3 + len(row_block)

    def spec(width, col, prev):
        def imap(bi, si, blk):
            r = jnp.maximum(blk - 1, 0) if prev else blk
            return (bi, si) + ((0, r) if len(row_block) == 2 else (r,)) + (col,)
        return pl.BlockSpec((None, None) + tuple(row_block) + (width,), imap)

    in_specs = [spec(wq, cq, False), spec(wk, ck, True), spec(wk, ck, False), spec(wk, cv, True), spec(wk, cv, False)]
    args = [arr] * 5
    if gqa:
        in_specs = [pl.BlockSpec(memory_space=pltpu.SMEM)] + in_specs
        args = [sinks] + args
    out_map = lambda bi, si, blk: (bi, si, blk, 0)
    n_out = 1 if gqa else 2
    outs = pl.pallas_call(
        functools.partial(_band_attn_kernel, il=il, gqa=gqa),
        grid=(b, n_seq, nblk),
        in_specs=in_specs,
        out_specs=[pl.BlockSpec((None, None, BLOCK, wq), out_map)] * n_out,
        out_shape=[jax.ShapeDtypeStruct((b, n_seq, seq_len, wq), _f32)] * n_out,
        compiler_params=_cparams(("arbitrary", "arbitrary", "arbitrary")),
        name="band_attn_a" if gqa else f"band_attn_b_il{il}_{n_seq}",
    )(*args)
    return outs


def _pad_rows(x, rows):
    return jnp.concatenate([x, jnp.zeros((rows - x.shape[0], x.shape[1]), x.dtype)], axis=0)


def _shift_in(cache, new_t, steps):
    lane = _lane_iota((1, LANES))
    keep = lane < LANES - steps
    ncol = cache.shape[1] // LANES
    cols = []
    prev = pltpu.roll(cache[:, :LANES], LANES - steps, axis=1)
    for c in range(ncol):
        nxt = new_t if c == ncol - 1 else pltpu.roll(cache[:, (c + 1) * LANES:(c + 2) * LANES], LANES - steps, axis=1)
        cols.append(jnp.where(keep, prev, nxt))
        prev = nxt
    return jnp.concatenate(cols, axis=1) if ncol > 1 else cols[0]


def _new_cols(new_rows, bi, s_len):
    t = jnp.transpose(_pad_rows(new_rows, LANES))
    return pltpu.roll(t, (LANES - s_len - bi * s_len) % LANES, axis=1)


def _sample_a_kernel(sink_ref, q_ref, kvn_ref, kc_ref, vc_ref, o_ref, ko_ref, vo_ref, *, s_len):
    nb = 8 // s_len
    q8 = q_ref[...]
    kn8 = kvn_ref[:, :LANES]
    vn8 = kvn_ref[:, LANES:]
    lo = _half_masks()
    row = lax.broadcasted_iota(jnp.int32, (8, 1), 0)
    scale = HEAD_DIM ** -0.5
    npair = GROUP_A // 2
    rows_g = npair * 16
    qrow = lax.broadcasted_iota(jnp.int32, (rows_g, LANES), 0) % 8
    lane = _lane_iota((rows_g, LANES))
    qb, qs_ = qrow // s_len, qrow % s_len
    out = jnp.zeros((8, WIDTH_A), _f32)
    for bi in range(nb):
        mine = (row // s_len) == bi
        kst = kc_ref[bi].reshape(N_KV_A * HEAD_DIM, WINDOW_A)
        vst = vc_ref[bi].reshape(N_KV_A * HEAD_DIM, WINDOW_A)
        ko_ref[bi] = _shift_in(kst, _new_cols(kn8, bi, s_len), s_len).reshape(N_KV_A, HEAD_DIM, WINDOW_A)
        vo_ref[bi] = _shift_in(vst, _new_cols(vn8, bi, s_len), s_len).reshape(N_KV_A, HEAD_DIM, WINDOW_A)
        kpos = PAST_LEN - WINDOW_A + lane
        d_c = PAST_LEN + qs_ - kpos
        mask_c = (d_c >= 0) & (d_c <= WINDOW_A) & (kpos >= 0)
        d_n = qs_ - lane % s_len
        mask_n = (lane < 8) & ((lane // s_len) == bi) & (d_n >= 0) & (d_n <= WINDOW_A)
        cols = []
        for g in range(N_KV_A):
            kdup = jnp.concatenate([kst[g * HEAD_DIM:(g + 1) * HEAD_DIM]] * 2, axis=0).astype(_bf16)
            vdup = jnp.concatenate([vst[g * HEAD_DIM:(g + 1) * HEAD_DIM]] * 2, axis=0).astype(_bf16)
            kn = _pad_rows(_dup_half(kn8, g), LANES).astype(_bf16)
            vn = _pad_rows(_dup_half(vn8, g), LANES).astype(_bf16)
            parts = []
            for pp in range(npair):
                hp = g * npair + pp
                qp = jnp.where(mine, q8[:, hp * LANES:(hp + 1) * LANES], 0.0)
                parts += [jnp.where(lo, qp, 0.0), jnp.where(lo, 0.0, qp)]
            qs = jnp.concatenate(parts, axis=0).astype(_bf16)
            s_c = jnp.dot(qs, kdup, preferred_element_type=_f32) * scale
            s_n = lax.dot_general(qs, kn, (((1,), (1,)), ((), ())), preferred_element_type=_f32) * scale
            s_c = jnp.where(mask_c, s_c, NEG_INF)
            s_n = jnp.where(mask_n, s_n, NEG_INF)
            m = jnp.maximum(jnp.max(s_c, axis=-1, keepdims=True), jnp.max(s_n, axis=-1, keepdims=True))
            p_c, p_n = jnp.exp(s_c - m), jnp.exp(s_n - m)
            den = jnp.sum(p_c, axis=-1, keepdims=True) + jnp.sum(p_n, axis=-1, keepdims=True)
            o = lax.dot_general(p_c.astype(_bf16), vdup, (((1,), (1,)), ((), ())), preferred_element_type=_f32)
            o = (o + jnp.dot(p_n.astype(_bf16), vn, preferred_element_type=_f32)) / den
            lse = m + jnp.log(den)
            for pp in range(npair):
                h0 = 2 * (g * npair + pp)
                r0 = pp * 16
                o0 = o[r0:r0 + 8] * jax.nn.sigmoid(lse[r0:r0 + 8] - sink_ref[h0])
                o1 = o[r0 + 8:r0 + 16] * jax.nn.sigmoid(lse[r0 + 8:r0 + 16] - sink_ref[h0 + 1])
                cols.append(jnp.where(lo, o0, o1))
        out = jnp.where(mine, jnp.concatenate(cols, axis=1), out)
    o_ref[...] = out


def _sample_attention_a(ps, kc_t, vc_t, sinks, s_len):
    n = ps.shape[0]
    nb = 8 // s_len
    bd = n // s_len
    cache_spec = pl.BlockSpec((nb, N_KV_A, HEAD_DIM, WINDOW_A), lambda i: (i, 0, 0, 0))
    return pl.pallas_call(
        functools.partial(_sample_a_kernel, s_len=s_len),
        grid=(n // 8,),
        in_specs=[
            pl.BlockSpec(memory_space=pltpu.SMEM),
            pl.BlockSpec((8, WIDTH_A), lambda i: (i, 0)),
            pl.BlockSpec((8, 2 * KV_WIDTH_A), lambda i: (i, COL_KA // (2 * KV_WIDTH_A))),
            cache_spec, cache_spec,
        ],
        out_specs=[pl.BlockSpec((8, WIDTH_A), lambda i: (i, 0)), cache_spec, cache_spec],
        out_shape=[
            jax.ShapeDtypeStruct((n, WIDTH_A), _f32),
            jax.ShapeDtypeStruct((bd, N_KV_A, HEAD_DIM, WINDOW_A), _f32),
            jax.ShapeDtypeStruct((bd, N_KV_A, HEAD_DIM, WINDOW_A), _f32),
        ],
        compiler_params=_cparams(("arbitrary",)),
        name="sample_attn_a",
    )(sinks, ps, ps, kc_t, vc_t)


def _sample_b_kernel(q_ref, kn_ref, vn_ref, kc_ref, vc_ref, o_ref, ko_ref, vo_ref, *, s_len):
    nb = 8 // s_len
    q8 = q_ref[...]
    kn8 = kn_ref[...]
    vn8 = vn_ref[...]
    lo = _half_masks()
    row = lax.broadcasted_iota(jnp.int32, (8, 1), 0)
    scale = HEAD_DIM ** -0.5
    qrow = lax.broadcasted_iota(jnp.int32, (16, 1), 0) % 8
    qs_ = qrow % s_len
    kn = _pad_rows(kn8, LANES).astype(_bf16)
    vn = _pad_rows(vn8, LANES).astype(_bf16)
    lane_n = _lane_iota((16, LANES))
    out = jnp.zeros((8, LANES), _f32)
    for bi in range(nb):
        mine = (row // s_len) == bi
        kst = kc_ref[bi].reshape(LANES, WINDOW_B)
        vst = vc_ref[bi].reshape(LANES, WINDOW_B)
        ko_ref[bi] = _shift_in(kst, _new_cols(kn8, bi, s_len), s_len).reshape(2, HEAD_DIM, WINDOW_B)
        vo_ref[bi] = _shift_in(vst, _new_cols(vn8, bi, s_len), s_len).reshape(2, HEAD_DIM, WINDOW_B)
        qp = jnp.where(mine, q8, 0.0)
        qs = jnp.concatenate([jnp.where(lo, qp, 0.0), jnp.where(lo, 0.0, qp)], axis=0).astype(_bf16)
        s_c = jnp.dot(qs, kst.astype(_bf16), preferred_element_type=_f32) * scale
        s_n = lax.dot_general(qs, kn, (((1,), (1,)), ((), ())), preferred_element_type=_f32) * scale
        d_n = qs_ - lane_n % s_len
        ok_n = (lane_n < 8) & ((lane_n // s_len) == bi) & (d_n >= 0)
        branches = []
        for w, r in DILATED_PAIRS:
            lo_i = WINDOW_B - w
            lane_c = lo_i + _lane_iota((16, w))
            d_c = WINDOW_B + qs_ - lane_c
            ok_c = (d_c % r == 0) & (d_c <= w) & (PAST_LEN - WINDOW_B + lane_c >= 0)
            sc = jnp.where(ok_c, s_c[:, lo_i:], NEG_INF)
            sn = jnp.where(ok_n & (d_n % r == 0) & (d_n <= w), s_n, NEG_INF)
            m = jnp.maximum(jnp.max(sc, axis=-1, keepdims=True), jnp.max(sn, axis=-1, keepdims=True))
            branches.append((lo_i, sc, sn, m))
        m_all = functools.reduce(jnp.maximum, [br[3] for br in branches])
        p_c = jnp.zeros((16, WINDOW_B), _f32)
        p_n = jnp.zeros((16, LANES), _f32)
        for lo_i, sc, sn, m in branches:
            e = jnp.exp(sc - m_all)
            p_c = p_c + (jnp.concatenate([jnp.zeros((16, lo_i), _f32), e], axis=1) if lo_i else e)
            p_n = p_n + jnp.exp(sn - m_all)
        den = jnp.sum(p_c, axis=-1, keepdims=True) + jnp.sum(p_n, axis=-1, keepdims=True)
        o = lax.dot_general(p_c.astype(_bf16), vst.astype(_bf16), (((1,), (1,)), ((), ())), preferred_element_type=_f32)
        o = (o + jnp.dot(p_n.astype(_bf16), vn, preferred_element_type=_f32)) / den
        out = jnp.where(mine, jnp.where(lo, o[:8], o[8:]), out)
    o_ref[...] = out


def _sample_attention_b(ps, kc_t, vc_t, s_len):
    n = ps.shape[0]
    nb = 8 // s_len
    bd = n // s_len
    npair = N_HEADS_B // 2
    cache_spec = pl.BlockSpec((nb, 2, HEAD_DIM, WINDOW_B), lambda i, hp: (i, hp, 0, 0))
    col = lambda base: (lambda i, hp: (i, base // LANES + hp))
    return pl.pallas_call(
        functools.partial(_sample_b_kernel, s_len=s_len),
        grid=(n // 8, npair),
        in_specs=[
            pl.BlockSpec((8, LANES), col(COL_QB)),
            pl.BlockSpec((8, LANES), col(COL_KB)),
            pl.BlockSpec((8, LANES), col(COL_VB)),
            cache_spec, cache_spec,
        ],
        out_specs=[pl.BlockSpec((8, LANES), lambda i, hp: (i, hp)), cache_spec, cache_spec],
        out_shape=[
            jax.ShapeDtypeStruct((n, WIDTH_B), _f32),
            jax.ShapeDtypeStruct((bd, N_HEADS_B, HEAD_DIM, WINDOW_B), _f32),
            jax.ShapeDtypeStruct((bd, N_HEADS_B, HEAD_DIM, WINDOW_B), _f32),
        ],
        compiler_params=_cparams(("arbitrary", "arbitrary")),
        name="sample_attn_b",
    )(ps, ps, ps, kc_t, vc_t)


def _rms(x, g):
    return x * lax.rsqrt(jnp.mean(x * x, axis=-1, keepdims=True) + RMS_EPS) * g


def _route(z):
    lane = _lane_iota(z.shape)
    big = jnp.int32(1 << 20)
    glane = lane < N_GROUPS
    gmax = jnp.max(jnp.where(glane, z, -jnp.inf), axis=-1, keepdims=True)
    grp = jnp.min(jnp.where(glane & (z == gmax), lane, big), axis=-1, keepdims=True)
    p_grp = 1.0 / jnp.sum(jnp.where(glane, jnp.exp(z - gmax), 0.0), axis=-1, keepdims=True)
    elane = (lane >= N_GROUPS) & (lane < N_GROUPS + N_EXPERTS) & ((lane - N_GROUPS) // EXPERTS_PER_GROUP == grp)
    t1 = jnp.max(jnp.where(elane, z, -jnp.inf), axis=-1, keepdims=True)
    i1 = jnp.min(jnp.where(elane & (z == t1), lane, big), axis=-1, keepdims=True)
    elane2 = elane & (lane != i1)
    t2 = jnp.max(jnp.where(elane2, z, -jnp.inf), axis=-1, keepdims=True)
    i2 = jnp.min(jnp.where(elane2 & (z == t2), lane, big), axis=-1, keepdims=True)
    e = jnp.exp(t2 - t1)
    w1 = p_grp * (1.0 / (1.0 + e))
    w2 = p_grp * (e / (1.0 + e))
    experts = jnp.where(lane == 0, i1 - N_GROUPS, jnp.where(lane == 1, i2 - N_GROUPS, 0))
    weights = jnp.where(lane == 0, w1, jnp.where(lane == 1, w2, 0.0))
    return experts, weights


def _store_row_tiles(ref, val):
    t, c = val.shape
    per = c // LANES
    for j in range(per):
        ref[pl.ds(j, t, stride=per), :] = val[:, j * LANES:(j + 1) * LANES]


def _load_row_tiles(ref, t):
    per = ref.shape[0] // t
    return [ref[pl.ds(j, t, stride=per), :] for j in range(per)]


def _merge_kernel(*refs, n_branch):
    n_lse = n_branch if n_branch > 1 else 0
    oa_ref = refs[0]
    ob_refs = refs[1:1 + n_branch]
    lse_refs = refs[1 + n_branch:1 + n_branch + n_lse]
    x_ref, ga_ref, gb_ref, wo_ref, gf_ref, wr_ref, br_ref = refs[1 + n_branch + n_lse:8 + n_branch + n_lse]
    x1_ref, hp_ref, re_ref, rw_ref = refs[-4:]
    if n_branch == 1:
        ob = ob_refs[0][...]
    else:
        lses = [r[...] for r in lse_refs]
        lmax = functools.reduce(jnp.maximum, lses)
        es = [jnp.exp(l - lmax) for l in lses]
        tot = functools.reduce(jnp.add, es)
        ob = functools.reduce(jnp.add, [(e / tot) * r[...] for e, r in zip(es, ob_refs)])
    o = jnp.concatenate([_rms(oa_ref[...], ga_ref[...]), _rms(ob, gb_ref[...])], axis=-1).astype(_bf16)
    x1 = x_ref[...] + jnp.dot(o, wo_ref[...], preferred_element_type=_f32)
    x1_ref[...] = x1
    h2 = _rms(x1, gf_ref[...])
    h2b = h2.astype(_bf16)
    z = jnp.dot(h2b, wr_ref[...], preferred_element_type=_f32) + br_ref[...]
    experts, weights = _route(z)
    re_ref[...] = experts
    rw_ref[...] = weights
    bits = pltpu.bitcast(h2b.astype(_f32), jnp.uint32)
    half = D_MODEL // 2
    _store_row_tiles(hp_ref, bits[:, :half] | (bits[:, half:] >> 16))


PACK_TILES = D_MODEL // 2 // LANES
ROW_TILES = D_MODEL // LANES


def _merge(oa, obs, lses, x, ga, gb, wo_bf16, gf, wr_bf16, br):
    n = oa.shape[0]
    nb = len(obs)
    assert len(lses) == (nb if nb > 1 else 0)
    row = lambda r, w: pl.BlockSpec((r, w), lambda i: (i, 0))
    full = lambda a: pl.BlockSpec(a.shape, lambda i: (0,) * a.ndim)
    consts = [ga, gb, wo_bf16, gf, wr_bf16, br]
    return pl.pallas_call(
        functools.partial(_merge_kernel, n_branch=nb),
        grid=(n // ROW_TM,),
        in_specs=([row(ROW_TM, WIDTH_A)] + [row(ROW_TM, WIDTH_B)] * (nb + len(lses)) + [row(ROW_TM, D_MODEL)]
                  + [full(a) for a in consts]),
        out_specs=[row(ROW_TM, D_MODEL), row(ROW_TM * PACK_TILES, LANES), row(ROW_TM, LANES), row(ROW_TM, LANES)],
        out_shape=[jax.ShapeDtypeStruct((n, D_MODEL), _f32),
                   jax.ShapeDtypeStruct((n * PACK_TILES, LANES), jnp.uint32),
                   jax.ShapeDtypeStruct((n, LANES), jnp.int32),
                   jax.ShapeDtypeStruct((n, LANES), _f32)],
        compiler_params=_cparams(("arbitrary",)),
        name=f"merge_{nb}",
    )(oa, *obs, *lses, x, *consts)


def _rank_kernel(re_ref, rank_ref, cnt_ref, carry):
    @pl.when(pl.program_id(0) == 0)
    def _():
        carry[...] = jnp.zeros_like(carry)

    e = re_ref[...]
    lane = _lane_iota(e.shape)
    r_i = lax.broadcasted_iota(jnp.int32, (ROW_TM, ROW_TM), 0)
    c_i = lax.broadcasted_iota(jnp.int32, (ROW_TM, ROW_TM), 1)
    before = (c_i < r_i).astype(_bf16)
    base = carry[...]
    ranks = []
    for k in range(2):
        onehot = lane == e[:, k:k + 1]
        oh = onehot.astype(_f32)
        earlier = jnp.dot(before, onehot.astype(_bf16), preferred_element_type=_f32)
        ranks.append(jnp.sum(oh * (earlier + base), axis=-1, keepdims=True))
        base = base + jnp.sum(oh, axis=0, keepdims=True)
    carry[...] = base
    cnt_ref[...] = base
    rank_ref[...] = jnp.where(lane == 0, ranks[0], jnp.where(lane == 1, ranks[1], 0.0)).astype(jnp.int32)


def _rank(route_e):
    n = route_e.shape[0]
    return pl.pallas_call(
        _rank_kernel,
        grid=(n // ROW_TM,),
        in_specs=[pl.BlockSpec((ROW_TM, LANES), lambda i: (i, 0))],
        out_specs=[pl.BlockSpec((ROW_TM, LANES), lambda i: (i, 0)), pl.BlockSpec((1, LANES), lambda i: (0, 0))],
        out_shape=[jax.ShapeDtypeStruct((n, LANES), jnp.int32), jax.ShapeDtypeStruct((1, LANES), _f32)],
        scratch_shapes=[pltpu.VMEM((1, LANES), _f32)],
        compiler_params=_cparams(("arbitrary",)),
        name="moe_rank",
    )(route_e)


PAIRS_PER_TILE = 2 * ROW_TM


def _row_copy(src, dst, i_src, i_dst, per, sem):
    return pltpu.make_async_copy(src.at[pl.ds(pl.multiple_of(i_src * per, per), per), :],
                                 dst.at[pl.ds(pl.multiple_of(i_dst * per, per), per), :], sem)


def _dispatch_kernel(dest_ref, src_ref, _init_ref, dst_ref, sem):
    def issue(p, c):
        _row_copy(src_ref, dst_ref, p // 2, dest_ref[p], PACK_TILES, sem).start()
        return c

    lax.fori_loop(0, PAIRS_PER_TILE, issue, 0, unroll=8)
    rows = ROW_TM * PACK_TILES
    for _ in range(2):
        pltpu.make_async_copy(src_ref, dst_ref.at[pl.ds(0, rows), :], sem).wait()


def _dispatch(dest_flat, src, out):
    anyspec = pl.BlockSpec(memory_space=pl.ANY)
    return pl.pallas_call(
        _dispatch_kernel,
        grid=(dest_flat.shape[0] // PAIRS_PER_TILE,),
        in_specs=[pl.BlockSpec((PAIRS_PER_TILE,), lambda i: (i,), memory_space=pltpu.SMEM),
                  pl.BlockSpec((ROW_TM * PACK_TILES, LANES), lambda i: (i, 0)), anyspec],
        out_specs=anyspec,
        out_shape=jax.ShapeDtypeStruct(out.shape, out.dtype),
        input_output_aliases={2: 0},
        scratch_shapes=[pltpu.SemaphoreType.DMA],
        compiler_params=pltpu.CompilerParams(dimension_semantics=("arbitrary",), has_side_effects=True,
                                             vmem_limit_bytes=VMEM_LIMIT),
        name="moe_dispatch",
    )(dest_flat, src, out)


def _ffn_kernel(te_ref, nt_ref, x_ref, wg_ref, wu_ref, wd_ref, y_ref):
    i = pl.program_id(0)

    @pl.when(i < nt_ref[0])
    def _():
        packed = _load_row_tiles(x_ref, MOE_TM)
        hi = [pltpu.bitcast(p & jnp.uint32(0xFFFF0000), _f32) for p in packed]
        lo = [pltpu.bitcast(p << 16, _f32) for p in packed]
        x = jnp.concatenate(hi + lo, axis=1).astype(_bf16)
        g = jnp.dot(x, wg_ref[...], preferred_element_type=_f32)
        u = jnp.dot(x, wu_ref[...], preferred_element_type=_f32)
        h = (jax.nn.silu(g) * u).astype(_bf16)
        _store_row_tiles(y_ref, jnp.dot(h, wd_ref[...], preferred_element_type=_f32))

    @pl.when(i >= nt_ref[0])
    def _():
        y_ref[...] = jnp.zeros_like(y_ref)


def _expert_ffn(tile_expert, n_tiles_used, x_disp, wg, wu, wd):
    rows = x_disp.shape[0] // PACK_TILES
    return pl.pallas_call(
        _ffn_kernel,
        grid_spec=pltpu.PrefetchScalarGridSpec(
            num_scalar_prefetch=2,
            grid=(rows // MOE_TM,),
            in_specs=[
                pl.BlockSpec((MOE_TM * PACK_TILES, LANES), lambda i, te, nt: (i, 0)),
                pl.BlockSpec((None, D_MODEL, D_EXPERT), lambda i, te, nt: (te[i], 0, 0)),
                pl.BlockSpec((None, D_MODEL, D_EXPERT), lambda i, te, nt: (te[i], 0, 0)),
                pl.BlockSpec((None, D_EXPERT, D_MODEL), lambda i, te, nt: (te[i], 0, 0)),
            ],
            out_specs=pl.BlockSpec((MOE_TM * ROW_TILES, LANES), lambda i, te, nt: (i, 0)),
        ),
        out_shape=jax.ShapeDtypeStruct((rows * ROW_TILES, LANES), _f32),
        compiler_params=_cparams(("arbitrary",)),
        name="moe_ffn",
    )(tile_expert, n_tiles_used, x_disp, wg, wu, wd)


def _final_kernel(dcur_ref, dnext_ref, x1_ref, rw_ref, g_ref, y_hbm, o_ref, ybuf, sems, *, n_tiles):
    i = pl.program_id(0)
    slot = i % 2

    def copy(dref, p, s):
        return _row_copy(y_hbm, ybuf.at[s, p % 2], dref[p], p // 2, ROW_TILES, sems.at[s])

    def issue(dref, s):
        def body(p, c):
            copy(dref, p, s).start()
            return c
        lax.fori_loop(0, PAIRS_PER_TILE, body, 0, unroll=8)

    @pl.when(i == 0)
    def _():
        issue(dcur_ref, 0)

    @pl.when(i + 1 < n_tiles)
    def _():
        issue(dnext_ref, 1 - slot)

    for k in range(2):
        pltpu.make_async_copy(y_hbm.at[pl.ds(0, ROW_TM * ROW_TILES), :], ybuf.at[slot, k], sems.at[slot]).wait()
    rw = rw_ref[...]
    y0 = jnp.concatenate(_load_row_tiles(ybuf.at[slot, 0], ROW_TM), axis=1)
    y1 = jnp.concatenate(_load_row_tiles(ybuf.at[slot, 1], ROW_TM), axis=1)
    moe = y0 * rw[:, 0:1] + y1 * rw[:, 1:2]
    o_ref[...] = _rms(x1_ref[...] + moe, g_ref[...])


def _final(dest_flat, x1, rw, g, y_disp):
    n = x1.shape[0]
    nt = n // ROW_TM
    dspec = lambda f: pl.BlockSpec((PAIRS_PER_TILE,), f, memory_space=pltpu.SMEM)
    return pl.pallas_call(
        functools.partial(_final_kernel, n_tiles=nt),
        grid=(nt,),
        in_specs=[
            dspec(lambda i: (i,)),
            dspec(lambda i: (jnp.minimum(i + 1, nt - 1),)),
            pl.BlockSpec((ROW_TM, D_MODEL), lambda i: (i, 0)),
            pl.BlockSpec((ROW_TM, LANES), lambda i: (i, 0)),
            pl.BlockSpec((1, D_MODEL), lambda i: (0, 0)),
            pl.BlockSpec(memory_space=pl.ANY),
        ],
        out_specs=pl.BlockSpec((ROW_TM, D_MODEL), lambda i: (i, 0)),
        out_shape=jax.ShapeDtypeStruct((n, D_MODEL), _f32),
        scratch_shapes=[pltpu.VMEM((2, 2, ROW_TM * ROW_TILES, LANES), _f32), pltpu.SemaphoreType.DMA((2,))],
        compiler_params=_cparams(("arbitrary",)),
        name="moe_combine_final",
    )(dest_flat, dest_flat, x1, rw, g, y_disp)


def _moe_plan(route_e, rank, counts):
    counts = counts[0, :N_EXPERTS].astype(jnp.int32)
    padded = (counts + MOE_TM - 1) // MOE_TM * MOE_TM
    pend = jnp.cumsum(padded)
    pstart = pend - padded
    e = route_e[:, :2]
    dest = (pstart[e] + rank[:, :2]).reshape(-1)
    n_pairs = dest.shape[0]
    n_tiles = (n_pairs + N_EXPERTS * (MOE_TM - 1) + MOE_TM - 1) // MOE_TM
    tile_start = jnp.arange(n_tiles, dtype=jnp.int32) * MOE_TM
    tile_expert = jnp.minimum(jnp.sum(pend[None, :] <= tile_start[:, None], axis=1), N_EXPERTS - 1).astype(jnp.int32)
    n_used = (pend[-1] // MOE_TM).astype(jnp.int32).reshape(1)
    return dest, tile_expert, n_used, n_tiles


def kernel(x_prompt, x_sample, cache_a_k, cache_a_v, cache_b_k, cache_b_v, g_attn, w_in, attn_sinks, g_out_a,
           g_out_b, w_out, g_ffn, w_router_group, b_router_group, w_router_expert, b_router_expert, w_gate, w_up,
           w_down, g_final):
    depth = g_attn.shape[0]
    assert depth == 1
    b, t, _ = x_prompt.shape
    bd, s_len, _ = x_sample.shape
    assert t % (RES * BLOCK) == 0 and t % PROJ_TM == 0 and 8 % s_len == 0
    n_p, n_s = b * t, bd * s_len
    n_tok = n_p + n_s
    l = 0

    w_l = w_in[l]
    w_in_b = jnp.concatenate([w_l[:, :WIDTH_A], w_l[:, WIDTH_A + 2 * KV_WIDTH_A:],
                              w_l[:, WIDTH_A:WIDTH_A + 2 * KV_WIDTH_A]], axis=1).astype(_bf16)
    w_out_b = w_out[l].astype(_bf16)
    wg_b, wu_b, wd_b = w_gate[l].astype(_bf16), w_up[l].astype(_bf16), w_down[l].astype(_bf16)
    pad = LANES - N_GROUPS - N_EXPERTS
    w_r = jnp.concatenate([w_router_group[l], w_router_expert[l], jnp.zeros((D_MODEL, pad), _f32)], axis=1)
    b_r = jnp.concatenate([b_router_group[l], b_router_expert[l], jnp.zeros((pad,), _f32)])[None]
    g_a, g_oa, g_ob, g_f = g_attn[l][None], g_out_a[l][None], g_out_b[l][None], g_ffn[l][None]
    sinks = attn_sinks[l]

    cos_p, sin_p = _rope_tables(jnp.arange(t))
    p_nat, kv_f32, p_res = _project_prompt(x_prompt, g_a, w_in_b, cos_p, sin_p)
    (oa,) = _band_attention(p_nat[:, None], 1, t, (BLOCK,), (0, COL_KA // LANES, COL_VA // LANES), il=1, sinks=sinks)
    cb = (COL_QB // WIDTH_B, COL_KB // WIDTH_B, COL_VB // WIDTH_B)
    o1, l1 = _band_attention(p_nat[:, None], 1, t, (BLOCK,), cb, il=1)
    o4, l4 = _band_attention(p_res.reshape(b, 4, 4, t // RES, 3 * WIDTH_B), 4, t // 4, (4, BLOCK // 4), (0, 1, 2), il=4)
    o16, l16 = _band_attention(p_res, RES, t // RES, (BLOCK,), (0, 1, 2), il=1)

    def from_res4(a):
        a = a.reshape(b, 4, t // RES // (BLOCK // 4), 4, BLOCK // 4, WIDTH_B)
        return a.transpose(0, 2, 4, 3, 1, 5).reshape(n_p, WIDTH_B)

    def from_res16(a):
        a = a.reshape(b, 4, 4, t // RES, WIDTH_B)
        return a.transpose(0, 3, 2, 1, 4).reshape(n_p, WIDTH_B)

    obs_p = [o1.reshape(n_p, WIDTH_B), from_res4(o4), from_res16(o16)]
    lses_p = [l1.reshape(n_p, WIDTH_B), from_res4(l4), from_res16(l16)]

    cos_s, sin_s = _rope_tables(PAST_LEN + jnp.arange(n_s) % s_len)
    ps = _project_sample(x_sample.reshape(n_s, D_MODEL), g_a, w_in_b, cos_s, sin_s)
    to_t = lambda c: jnp.transpose(c[l], (0, 2, 3, 1))
    from_t = lambda c: jnp.transpose(c, (0, 3, 1, 2))[None]
    oa_s, ak_s, av_s = _sample_attention_a(ps, to_t(cache_a_k), to_t(cache_a_v), sinks, s_len)
    ob_s, bk_s, bv_s = _sample_attention_b(ps, to_t(cache_b_k), to_t(cache_b_v), s_len)

    consts = (g_oa, g_ob, w_out_b, g_f, w_r.astype(_bf16), b_r)
    x1_p, h2_p, re_p, rw_p = _merge(oa.reshape(n_p, WIDTH_A), obs_p, lses_p, x_prompt.reshape(n_p, D_MODEL), *consts)
    x1_s, h2_s, re_s, rw_s = _merge(oa_s, [ob_s], [], x_sample.reshape(n_s, D_MODEL), *consts)

    route_e = jnp.concatenate([re_p, re_s], axis=0)
    rank, counts = _rank(route_e)
    dest, tile_expert, n_used, n_tiles = _moe_plan(route_e, rank, counts)
    dest_p, dest_s = dest[:2 * n_p], dest[2 * n_p:]
    x_disp = jnp.zeros((n_tiles * MOE_TM * PACK_TILES, LANES), jnp.uint32)
    x_disp = _dispatch(dest_p, h2_p, x_disp)
    x_disp = _dispatch(dest_s, h2_s, x_disp)
    y_disp = _expert_ffn(tile_expert, n_used, x_disp, wg_b, wu_b, wd_b)
    g_fin = g_final[None]
    y_prompt = _final(dest_p, x1_p, rw_p, g_fin, y_disp).reshape(b, t, D_MODEL)
    y_sample = _final(dest_s, x1_s, rw_s, g_fin, y_disp).reshape(bd, s_len, D_MODEL)

    kv = kv_f32
    heads = lambda a, h: a.reshape(b, -1, h, HEAD_DIM)[None]
    bk_p = heads(kv[:, t - WINDOW_B:, 0:WIDTH_B], N_HEADS_B)
    bv_p = heads(kv[:, t - WINDOW_B:, WIDTH_B:2 * WIDTH_B], N_HEADS_B)
    ak_p = heads(kv[:, t - WINDOW_A:, 2 * WIDTH_B:2 * WIDTH_B + KV_WIDTH_A], N_KV_A)
    av_p = heads(kv[:, t - WINDOW_A:, 2 * WIDTH_B + KV_WIDTH_A:], N_KV_A)
    return (y_prompt, y_sample, ak_p, av_p, bk_p, bv_p, from_t(ak_s), from_t(av_s), from_t(bk_s), from_t(bv_s))
```

```python
import functools

import jax
import jax.numpy as jnp
from jax import lax
from jax.experimental import pallas as pl
from jax.experimental.pallas import tpu as pltpu

D_MODEL = 2048
HEAD_DIM = 64
N_HEADS_A = 16
N_KV_A = 2
GROUP_A = N_HEADS_A // N_KV_A
N_HEADS_B = 16
WIDTH_A = N_HEADS_A * HEAD_DIM
WIDTH_B = N_HEADS_B * HEAD_DIM
KV_WIDTH_A = N_KV_A * HEAD_DIM
PROJ_WIDTH = WIDTH_A + 2 * KV_WIDTH_A + 3 * WIDTH_B
WINDOW_A = 128
DILATED_PAIRS = ((128, 1), (512, 4), (2048, 16))
WINDOW_B = 2048
BLOCK = 128
ROPE_THETA = 10000.0
N_GROUPS = 4
EXPERTS_PER_GROUP = 8
N_EXPERTS = N_GROUPS * EXPERTS_PER_GROUP
D_EXPERT = D_MODEL // 2
RMS_EPS = 1e-6
NEG_INF = -1e30
PAST_LEN = 16384

LANES = 128
VMEM_LIMIT = 56 * 1024 * 1024

COL_QA = 0
COL_QB = WIDTH_A
COL_KB = COL_QB + WIDTH_B
COL_VB = COL_KB + WIDTH_B
COL_KA = COL_VB + WIDTH_B
COL_VA = COL_KA + KV_WIDTH_A
KV_COLS = 2 * WIDTH_B + 2 * KV_WIDTH_A

PROJ_TM = 512
PROJ_TN = 256
RES = 16
ROW_TM = 256
MOE_TM = 256

_f32 = jnp.float32
_bf16 = jnp.bfloat16


def _cparams(sem):
    return pltpu.CompilerParams(dimension_semantics=sem, vmem_limit_bytes=VMEM_LIMIT)


def _lane_iota(shape):
    return lax.broadcasted_iota(jnp.int32, shape, len(shape) - 1)


def _rope_tables(pos):
    half = HEAD_DIM // 2
    inv_freq = ROPE_THETA ** (-jnp.arange(half, dtype=_f32) / half)
    ang = pos.astype(_f32)[:, None] * inv_freq[None, :]
    cos, sin = jnp.cos(ang), jnp.sin(ang)
    return jnp.tile(cos, (1, 4)), jnp.tile(jnp.concatenate([-sin, sin], axis=-1), (1, 2))


def _proj_kernel(x_ref, g_ref, w_ref, cos_ref, sin_ref, *refs, prompt):
    if prompt:
        p_ref, kv_ref, res_ref, h_scr, mm_scr, rope_scr = refs
    else:
        p_ref, h_scr, mm_scr = refs
    j = pl.program_id(2)
    jj = j - 1

    @pl.when(j == 0)
    def _():
        x = x_ref[...]
        ms = jnp.mean(x * x, axis=-1, keepdims=True)
        h_scr[...] = (x * lax.rsqrt(ms + RMS_EPS) * g_ref[...]).astype(_bf16)
        mm_scr[...] = jnp.zeros_like(mm_scr)

    lane = _lane_iota((1, LANES))
    first_half = (lane % HEAD_DIM) < (HEAD_DIM // 2)
    cos, sin = cos_ref[...], sin_ref[...]
    for half in range(PROJ_TN // LANES):
        cs = slice(half * LANES, (half + 1) * LANES)
        sb = jj * (PROJ_TN // LANES) + half
        is_v = ((sb >= COL_VB // LANES) & (sb < COL_KA // LANES)) | (sb >= COL_VA // LANES)
        a = mm_scr[:, cs]
        swapped = jnp.where(first_half, pltpu.roll(a, LANES - HEAD_DIM // 2, axis=1),
                            pltpu.roll(a, HEAD_DIM // 2, axis=1))
        y = a * jnp.where(is_v, 1.0, cos) + swapped * jnp.where(is_v, 0.0, sin)
        if prompt:
            p_ref[:, cs] = y.astype(_bf16)
            kv_ref[:, cs] = y
            rope_scr[half] = y
        else:
            p_ref[:, cs] = y
    mm_scr[...] = jnp.dot(h_scr[...], w_ref[...], preferred_element_type=_f32)
    if not prompt:
        return

    @pl.when((jj >= COL_QB // PROJ_TN) & (jj < COL_KA // PROJ_TN))
    def _():
        rows = PROJ_TM // RES
        for c in range(RES):
            c16 = 4 * (c % 4) + c // 4
            for half in range(PROJ_TN // LANES):
                res_ref[c, :, half * LANES:(half + 1) * LANES] = (
                    rope_scr[half, pl.ds(c16, rows, stride=RES), :].astype(_bf16))


PROJ_NJ = PROJ_WIDTH // PROJ_TN


def _proj_in_specs(x_spec):
    return [
        x_spec,
        pl.BlockSpec((1, D_MODEL), lambda bi, i, j: (0, 0)),
        pl.BlockSpec((D_MODEL, PROJ_TN), lambda bi, i, j: (0, jnp.minimum(j, PROJ_NJ - 1))),
        pl.BlockSpec((PROJ_TM, LANES), lambda bi, i, j: (i, 0)),
        pl.BlockSpec((PROJ_TM, LANES), lambda bi, i, j: (i, 0)),
    ]


def _project_prompt(x, g, w_bf16, cos, sin):
    b, t, _ = x.shape
    jkb, jqb = COL_KB // PROJ_TN, COL_QB // PROJ_TN
    n_res = 3 * WIDTH_B // PROJ_TN
    p_map = lambda bi, i, j: (bi, i, jnp.maximum(j - 1, 0))
    kv_map = lambda bi, i, j: (bi, i, jnp.maximum(j - 1 - jkb, 0))
    res_map = lambda bi, i, j: (bi, 0, i, jnp.clip(j - 1 - jqb, 0, n_res - 1))
    return pl.pallas_call(
        functools.partial(_proj_kernel, prompt=True),
        grid=(b, t // PROJ_TM, PROJ_NJ + 1),
        in_specs=_proj_in_specs(pl.BlockSpec((None, PROJ_TM, D_MODEL), lambda bi, i, j: (bi, i, 0))),
        out_specs=[
            pl.BlockSpec((None, PROJ_TM, PROJ_TN), p_map),
            pl.BlockSpec((None, PROJ_TM, PROJ_TN), kv_map),
            pl.BlockSpec((None, RES, PROJ_TM // RES, PROJ_TN), res_map),
        ],
        out_shape=[
            jax.ShapeDtypeStruct((b, t, PROJ_WIDTH), _bf16),
            jax.ShapeDtypeStruct((b, t, KV_COLS), _f32),
            jax.ShapeDtypeStruct((b, RES, t // RES, 3 * WIDTH_B), _bf16),
        ],
        scratch_shapes=[pltpu.VMEM((PROJ_TM, D_MODEL), _bf16), pltpu.VMEM((PROJ_TM, PROJ_TN), _f32),
                        pltpu.VMEM((PROJ_TN // LANES, PROJ_TM, LANES), _f32)],
        compiler_params=_cparams(("arbitrary", "arbitrary", "arbitrary")),
        name="proj_prompt",
    )(x, g, w_bf16, cos, sin)


def _project_sample(x, g, w_bf16, cos, sin):
    n = x.shape[0]
    return pl.pallas_call(
        functools.partial(_proj_kernel, prompt=False),
        grid=(1, n // PROJ_TM, PROJ_NJ + 1),
        in_specs=_proj_in_specs(pl.BlockSpec((PROJ_TM, D_MODEL), lambda bi, i, j: (i, 0))),
        out_specs=pl.BlockSpec((PROJ_TM, PROJ_TN), lambda bi, i, j: (i, jnp.maximum(j - 1, 0))),
        out_shape=jax.ShapeDtypeStruct((n, PROJ_WIDTH), _f32),
        scratch_shapes=[pltpu.VMEM((PROJ_TM, D_MODEL), _bf16), pltpu.VMEM((PROJ_TM, PROJ_TN), _f32)],
        compiler_params=_cparams(("arbitrary", "arbitrary", "arbitrary")),
        name="proj_sample",
    )(x, g, w_bf16, cos, sin)


def _half_masks():
    lane = _lane_iota((1, LANES))
    return lane < HEAD_DIM


def _dup_half(x, g):
    lo = _half_masks()
    r = pltpu.roll(x, HEAD_DIM, axis=x.ndim - 1)
    return jnp.where(lo, x, r) if g == 0 else jnp.where(lo, r, x)


def _band_attn_kernel(*refs, il, gqa):
    if gqa:
        sink_ref, q_ref, kp_ref, kc_ref, vp_ref, vc_ref, o_ref = refs
    else:
        q_ref, kp_ref, kc_ref, vp_ref, vc_ref, o_ref, lse_ref = refs
    blk = pl.program_id(2)
    q = q_ref[...].reshape(BLOCK, -1)
    k2 = jnp.concatenate([kp_ref[...].reshape(BLOCK, -1), kc_ref[...].reshape(BLOCK, -1)], axis=0)
    v2 = jnp.concatenate([vp_ref[...].reshape(BLOCK, -1), vc_ref[...].reshape(BLOCK, -1)], axis=0)

    per = BLOCK // il
    w = lambda idx: il * (idx % per) + idx // per
    wi = w(lax.broadcasted_iota(jnp.int32, (2 * BLOCK, BLOCK), 0) % BLOCK)
    wj = w(lax.broadcasted_iota(jnp.int32, (2 * BLOCK, BLOCK), 1))
    newer = wj > wi
    same = wj == wi
    prev_bias = jnp.where(blk > 0, 0.0, NEG_INF)

    lo = _half_masks()
    zero = jnp.zeros((), q.dtype)
    q = q * jnp.asarray(HEAD_DIM ** -0.5, q.dtype)
    if gqa:
        kf, vf = k2.astype(_f32), v2.astype(_f32)
        kdup = [_dup_half(kf, g).astype(_bf16) for g in range(N_KV_A)]
        vdup = [_dup_half(vf, g).astype(_bf16) for g in range(N_KV_A)]
    for hp in range(q.shape[1] // LANES):
        qp = q[:, hp * LANES:(hp + 1) * LANES]
        if gqa:
            kk = kdup[(2 * hp) // GROUP_A]
            vv = vdup[(2 * hp) // GROUP_A]
        else:
            kk = k2[:, hp * LANES:(hp + 1) * LANES]
            vv = v2[:, hp * LANES:(hp + 1) * LANES]
        qs = jnp.concatenate([jnp.where(lo, qp, zero), jnp.where(lo, zero, qp)], axis=0)
        s = lax.dot_general(qs, kk, (((1,), (1,)), ((), ())), preferred_element_type=_f32)
        s_prev = s[:, :BLOCK] + prev_bias
        s_band = jnp.where(newer, s_prev, s[:, BLOCK:])
        s_edge = jnp.sum(jnp.where(same, s_prev, 0.0), axis=-1, keepdims=True)
        m = jnp.maximum(jnp.max(s_band, axis=-1, keepdims=True), s_edge)
        e = jnp.exp(s_band - m)
        e_edge = jnp.exp(s_edge - m)
        den = jnp.sum(e, axis=-1, keepdims=True) + e_edge
        p_prev = jnp.where(newer, e, jnp.where(same, e_edge, 0.0))
        p_cur = jnp.where(newer, 0.0, e)
        p = jnp.concatenate([p_prev, p_cur], axis=1).astype(_bf16)
        o2 = jnp.dot(p, vv, preferred_element_type=_f32) / den
        lse = m + jnp.log(den)
        if gqa:
            o2 = jnp.concatenate([
                o2[:BLOCK] * jax.nn.sigmoid(lse[:BLOCK] - sink_ref[2 * hp]),
                o2[BLOCK:] * jax.nn.sigmoid(lse[BLOCK:] - sink_ref[2 * hp + 1])], axis=0)
        o_ref[:, hp * LANES:(hp + 1) * LANES] = jnp.where(lo, o2[:BLOCK], o2[BLOCK:])
        if not gqa:
            lse_ref[:, hp * LANES:(hp + 1) * LANES] = jnp.where(
                lo, jnp.broadcast_to(lse[:BLOCK], (BLOCK, LANES)), jnp.broadcast_to(lse[BLOCK:], (BLOCK, LANES)))


def _band_attention(arr, n_seq, seq_len, row_block, col_blocks, *, il, sinks=None):
    gqa = sinks is not None
    b = arr.shape[0]
    nblk = seq_len // BLOCK
    wq = WIDTH_A if gqa else WIDTH_B
    wk = LANES if gqa else WIDTH_B
    cq, ck, cv = col_blocks
    assert len(arr.shape) == 3 + len(row_block)

    def spec(width, col, prev):
        def imap(bi, si, blk):
            r = jnp.maximum(blk - 1, 0) if prev else blk
            return (bi, si) + ((0, r) if len(row_block) == 2 else (r,)) + (col,)
        return pl.BlockSpec((None, None) + tuple(row_block) + (width,), imap)

    in_specs = [spec(wq, cq, False), spec(wk, ck, True), spec(wk, ck, False), spec(wk, cv, True), spec(wk, cv, False)]
    args = [arr] * 5
    if gqa:
        in_specs = [pl.BlockSpec(memory_space=pltpu.SMEM)] + in_specs
        args = [sinks] + args
    out_map = lambda bi, si, blk: (bi, si, blk, 0)
    n_out = 1 if gqa else 2
    outs = pl.pallas_call(
        functools.partial(_band_attn_kernel, il=il, gqa=gqa),
        grid=(b, n_seq, nblk),
        in_specs=in_specs,
        out_specs=[pl.BlockSpec((None, None, BLOCK, wq), out_map)] * n_out,
        out_shape=[jax.ShapeDtypeStruct((b, n_seq, seq_len, wq), _f32)] * n_out,
        compiler_params=_cparams(("arbitrary", "arbitrary", "arbitrary")),
        name="band_attn_a" if gqa else f"band_attn_b_il{il}_{n_seq}",
    )(*args)
    return outs


def _pad_rows(x, rows):
    return jnp.concatenate([x, jnp.zeros((rows - x.shape[0], x.shape[1]), x.dtype)], axis=0)


def _shift_in(cache, new_t, steps):
    lane = _lane_iota((1, LANES))
    keep = lane < LANES - steps
    ncol = cache.shape[1] // LANES
    cols = []
    prev = pltpu.roll(cache[:, :LANES], LANES - steps, axis=1)
    for c in range(ncol):
        nxt = new_t if c == ncol - 1 else pltpu.roll(cache[:, (c + 1) * LANES:(c + 2) * LANES], LANES - steps, axis=1)
        cols.append(jnp.where(keep, prev, nxt))
        prev = nxt
    return jnp.concatenate(cols, axis=1) if ncol > 1 else cols[0]


def _new_cols(new_rows, bi, s_len):
    t = jnp.transpose(_pad_rows(new_rows, LANES))
    return pltpu.roll(t, (LANES - s_len - bi * s_len) % LANES, axis=1)


def _sample_a_kernel(sink_ref, q_ref, kvn_ref, kc_ref, vc_ref, o_ref, ko_ref, vo_ref, *, s_len):
    nb = 8 // s_len
    q8 = q_ref[...]
    kn8 = kvn_ref[:, :LANES]
    vn8 = kvn_ref[:, LANES:]
    lo = _half_masks()
    row = lax.broadcasted_iota(jnp.int32, (8, 1), 0)
    scale = HEAD_DIM ** -0.5
    npair = GROUP_A // 2
    rows_g = npair * 16
    qrow = lax.broadcasted_iota(jnp.int32, (rows_g, LANES), 0) % 8
    lane = _lane_iota((rows_g, LANES))
    qb, qs_ = qrow // s_len, qrow % s_len
    out = jnp.zeros((8, WIDTH_A), _f32)
    for bi in range(nb):
        mine = (row // s_len) == bi
        kst = kc_ref[bi].reshape(N_KV_A * HEAD_DIM, WINDOW_A)
        vst = vc_ref[bi].reshape(N_KV_A * HEAD_DIM, WINDOW_A)
        ko_ref[bi] = _shift_in(kst, _new_cols(kn8, bi, s_len), s_len).reshape(N_KV_A, HEAD_DIM, WINDOW_A)
        vo_ref[bi] = _shift_in(vst, _new_cols(vn8, bi, s_len), s_len).reshape(N_KV_A, HEAD_DIM, WINDOW_A)
        kpos = PAST_LEN - WINDOW_A + lane
        d_c = PAST_LEN + qs_ - kpos
        mask_c = (d_c >= 0) & (d_c <= WINDOW_A) & (kpos >= 0)
        d_n = qs_ - lane % s_len
        mask_n = (lane < 8) & ((lane // s_len) == bi) & (d_n >= 0) & (d_n <= WINDOW_A)
        cols = []
        for g in range(N_KV_A):
            kdup = jnp.concatenate([kst[g * HEAD_DIM:(g + 1) * HEAD_DIM]] * 2, axis=0).astype(_bf16)
            vdup = jnp.concatenate([vst[g * HEAD_DIM:(g + 1) * HEAD_DIM]] * 2, axis=0).astype(_bf16)
            kn = _pad_rows(_dup_half(kn8, g), LANES).astype(_bf16)
            vn = _pad_rows(_dup_half(vn8, g), LANES).astype(_bf16)
            parts = []
            for pp in range(npair):
                hp = g * npair + pp
                qp = jnp.where(mine, q8[:, hp * LANES:(hp + 1) * LANES], 0.0)
                parts += [jnp.where(lo, qp, 0.0), jnp.where(lo, 0.0, qp)]
            qs = jnp.concatenate(parts, axis=0).astype(_bf16)
            s_c = jnp.dot(qs, kdup, preferred_element_type=_f32) * scale
            s_n = lax.dot_general(qs, kn, (((1,), (1,)), ((), ())), preferred_element_type=_f32) * scale
            s_c = jnp.where(mask_c, s_c, NEG_INF)
            s_n = jnp.where(mask_n, s_n, NEG_INF)
            m = jnp.maximum(jnp.max(s_c, axis=-1, keepdims=True), jnp.max(s_n, axis=-1, keepdims=True))
            p_c, p_n = jnp.exp(s_c - m), jnp.exp(s_n - m)
            den = jnp.sum(p_c, axis=-1, keepdims=True) + jnp.sum(p_n, axis=-1, keepdims=True)
            o = lax.dot_general(p_c.astype(_bf16), vdup, (((1,), (1,)), ((), ())), preferred_element_type=_f32)
            o = (o + jnp.dot(p_n.astype(_bf16), vn, preferred_element_type=_f32)) / den
            lse = m + jnp.log(den)
            for pp in range(npair):
                h0 = 2 * (g * npair + pp)
                r0 = pp * 16
                o0 = o[r0:r0 + 8] * jax.nn.sigmoid(lse[r0:r0 + 8] - sink_ref[h0])
                o1 = o[r0 + 8:r0 + 16] * jax.nn.sigmoid(lse[r0 + 8:r0 + 16] - sink_ref[h0 + 1])
                cols.append(jnp.where(lo, o0, o1))
        out = jnp.where(mine, jnp.concatenate(cols, axis=1), out)
    o_ref[...] = out


def _sample_attention_a(ps, kc_t, vc_t, sinks, s_len):
    n = ps.shape[0]
    nb = 8 // s_len
    bd = n // s_len
    cache_spec = pl.BlockSpec((nb, N_KV_A, HEAD_DIM, WINDOW_A), lambda i: (i, 0, 0, 0))
    return pl.pallas_call(
        functools.partial(_sample_a_kernel, s_len=s_len),
        grid=(n // 8,),
        in_specs=[
            pl.BlockSpec(memory_space=pltpu.SMEM),
            pl.BlockSpec((8, WIDTH_A), lambda i: (i, 0)),
            pl.BlockSpec((8, 2 * KV_WIDTH_A), lambda i: (i, COL_KA // (2 * KV_WIDTH_A))),
            cache_spec, cache_spec,
        ],
        out_specs=[pl.BlockSpec((8, WIDTH_A), lambda i: (i, 0)), cache_spec, cache_spec],
        out_shape=[
            jax.ShapeDtypeStruct((n, WIDTH_A), _f32),
            jax.ShapeDtypeStruct((bd, N_KV_A, HEAD_DIM, WINDOW_A), _f32),
            jax.ShapeDtypeStruct((bd, N_KV_A, HEAD_DIM, WINDOW_A), _f32),
        ],
        compiler_params=_cparams(("arbitrary",)),
        name="sample_attn_a",
    )(sinks, ps, ps, kc_t, vc_t)


def _sample_b_kernel(q_ref, kn_ref, vn_ref, kc_ref, vc_ref, o_ref, ko_ref, vo_ref, *, s_len):
    nb = 8 // s_len
    q8 = q_ref[...]
    kn8 = kn_ref[...]
    vn8 = vn_ref[...]
    lo = _half_masks()
    row = lax.broadcasted_iota(jnp.int32, (8, 1), 0)
    scale = HEAD_DIM ** -0.5
    qrow = lax.broadcasted_iota(jnp.int32, (16, 1), 0) % 8
    qs_ = qrow % s_len
    kn = _pad_rows(kn8, LANES).astype(_bf16)
    vn = _pad_rows(vn8, LANES).astype(_bf16)
    lane_n = _lane_iota((16, LANES))
    out = jnp.zeros((8, LANES), _f32)
    for bi in range(nb):
        mine = (row // s_len) == bi
        kst = kc_ref[bi].reshape(LANES, WINDOW_B)
        vst = vc_ref[bi].reshape(LANES, WINDOW_B)
        ko_ref[bi] = _shift_in(kst, _new_cols(kn8, bi, s_len), s_len).reshape(2, HEAD_DIM, WINDOW_B)
        vo_ref[bi] = _shift_in(vst, _new_cols(vn8, bi, s_len), s_len).reshape(2, HEAD_DIM, WINDOW_B)
        qp = jnp.where(mine, q8, 0.0)
        qs = jnp.concatenate([jnp.where(lo, qp, 0.0), jnp.where(lo, 0.0, qp)], axis=0).astype(_bf16)
        s_c = jnp.dot(qs, kst.astype(_bf16), preferred_element_type=_f32) * scale
        s_n = lax.dot_general(qs, kn, (((1,), (1,)), ((), ())), preferred_element_type=_f32) * scale
        d_n = qs_ - lane_n % s_len
        ok_n = (lane_n < 8) & ((lane_n // s_len) == bi) & (d_n >= 0)
        branches = []
        for w, r in DILATED_PAIRS:
            lo_i = WINDOW_B - w
            lane_c = lo_i + _lane_iota((16, w))
            d_c = WINDOW_B + qs_ - lane_c
            ok_c = (d_c % r == 0) & (d_c <= w) & (PAST_LEN - WINDOW_B + lane_c >= 0)
            sc = jnp.where(ok_c, s_c[:, lo_i:], NEG_INF)
            sn = jnp.where(ok_n & (d_n % r == 0) & (d_n <= w), s_n, NEG_INF)
            m = jnp.maximum(jnp.max(sc, axis=-1, keepdims=True), jnp.max(sn, axis=-1, keepdims=True))
            branches.append((lo_i, sc, sn, m))
        m_all = functools.reduce(jnp.maximum, [br[3] for br in branches])
        p_c = jnp.zeros((16, WINDOW_B), _f32)
        p_n = jnp.zeros((16, LANES), _f32)
        for lo_i, sc, sn, m in branches:
            e = jnp.exp(sc - m_all)
            p_c = p_c + (jnp.concatenate([jnp.zeros((16, lo_i), _f32), e], axis=1) if lo_i else e)
            p_n = p_n + jnp.exp(sn - m_all)
        den = jnp.sum(p_c, axis=-1, keepdims=True) + jnp.sum(p_n, axis=-1, keepdims=True)
        o = lax.dot_general(p_c.astype(_bf16), vst.astype(_bf16), (((1,), (1,)), ((), ())), preferred_element_type=_f32)
        o = (o + jnp.dot(p_n.astype(_bf16), vn, preferred_element_type=_f32)) / den
        out = jnp.where(mine, jnp.where(lo, o[:8], o[8:]), out)
    o_ref[...] = out


def _sample_attention_b(ps, kc_t, vc_t, s_len):
    n = ps.shape[0]
    nb = 8 // s_len
    bd = n // s_len
    npair = N_HEADS_B // 2
    cache_spec = pl.BlockSpec((nb, 2, HEAD_DIM, WINDOW_B), lambda i, hp: (i, hp, 0, 0))
    col = lambda base: (lambda i, hp: (i, base // LANES + hp))
    return pl.pallas_call(
        functools.partial(_sample_b_kernel, s_len=s_len),
        grid=(n // 8, npair),
        in_specs=[
            pl.BlockSpec((8, LANES), col(COL_QB)),
            pl.BlockSpec((8, LANES), col(COL_KB)),
            pl.BlockSpec((8, LANES), col(COL_VB)),
            cache_spec, cache_spec,
        ],
        out_specs=[pl.BlockSpec((8, LANES), lambda i, hp: (i, hp)), cache_spec, cache_spec],
        out_shape=[
            jax.ShapeDtypeStruct((n, WIDTH_B), _f32),
            jax.ShapeDtypeStruct((bd, N_HEADS_B, HEAD_DIM, WINDOW_B), _f32),
            jax.ShapeDtypeStruct((bd, N_HEADS_B, HEAD_DIM, WINDOW_B), _f32),
        ],
        compiler_params=_cparams(("arbitrary", "arbitrary")),
        name="sample_attn_b",
    )(ps, ps, ps, kc_t, vc_t)


def _rms(x, g):
    return x * lax.rsqrt(jnp.mean(x * x, axis=-1, keepdims=True) + RMS_EPS) * g


def _route(z):
    lane = _lane_iota(z.shape)
    big = jnp.int32(1 << 20)
    glane = lane < N_GROUPS
    gmax = jnp.max(jnp.where(glane, z, -jnp.inf), axis=-1, keepdims=True)
    grp = jnp.min(jnp.where(glane & (z == gmax), lane, big), axis=-1, keepdims=True)
    p_grp = 1.0 / jnp.sum(jnp.where(glane, jnp.exp(z - gmax), 0.0), axis=-1, keepdims=True)
    elane = (lane >= N_GROUPS) & (lane < N_GROUPS + N_EXPERTS) & ((lane - N_GROUPS) // EXPERTS_PER_GROUP == grp)
    t1 = jnp.max(jnp.where(elane, z, -jnp.inf), axis=-1, keepdims=True)
    i1 = jnp.min(jnp.where(elane & (z == t1), lane, big), axis=-1, keepdims=True)
    elane2 = elane & (lane != i1)
    t2 = jnp.max(jnp.where(elane2, z, -jnp.inf), axis=-1, keepdims=True)
    i2 = jnp.min(jnp.where(elane2 & (z == t2), lane, big), axis=-1, keepdims=True)
    e = jnp.exp(t2 - t1)
    w1 = p_grp * (1.0 / (1.0 + e))
    w2 = p_grp * (e / (1.0 + e))
    experts = jnp.where(lane == 0, i1 - N_GROUPS, jnp.where(lane == 1, i2 - N_GROUPS, 0))
    weights = jnp.where(lane == 0, w1, jnp.where(lane == 1, w2, 0.0))
    return experts, weights


def _store_row_tiles(ref, val):
    t, c = val.shape
    per = c // LANES
    for j in range(per):
        ref[pl.ds(j, t, stride=per), :] = val[:, j * LANES:(j + 1) * LANES]


def _load_row_tiles(ref, t):
    per = ref.shape[0] // t
    return [ref[pl.ds(j, t, stride=per), :] for j in range(per)]


def _res_of(c):
    return 4 * (c % 4) + c // 4


def _tile_rows(ref):
    if len(ref.shape) == 2:
        return ref[...]
    return jnp.concatenate([ref[:, _res_of(c), :] for c in range(RES)], axis=0)


def _merge_kernel(*refs, n_branch, res_lse):
    n_lse = n_branch if n_branch > 1 else 0
    oa_ref = refs[0]
    ob_refs = refs[1:1 + n_branch]
    lse_refs = refs[1 + n_branch:1 + n_branch + n_lse]
    x_ref, ga_ref, gb_ref, wo_ref, gf_ref, wr_ref, br_ref = refs[1 + n_branch + n_lse:8 + n_branch + n_lse]
    x1_ref, hp_ref, re_ref, rw_ref = refs[-4:]
    rows = lambda r, k: r[...].reshape(ROW_TM, r.shape[-1]) if res_lse[k] else _tile_rows(r)
    if n_branch == 1:
        ob = rows(ob_refs[0], 0)
    else:
        lses = [rows(r, k) for k, r in enumerate(lse_refs)]
        lmax = functools.reduce(jnp.maximum, lses)
        es = [jnp.exp(l - lmax) for l in lses]
        tot = functools.reduce(jnp.add, es)
        ob = functools.reduce(jnp.add, [(e / tot) * rows(r, k) for k, (e, r) in enumerate(zip(es, ob_refs))])
    o = jnp.concatenate([_rms(_tile_rows(oa_ref), ga_ref[...]), _rms(ob, gb_ref[...])], axis=-1).astype(_bf16)
    x1 = _tile_rows(x_ref) + jnp.dot(o, wo_ref[...], preferred_element_type=_f32)
    x1_ref[...] = x1
    h2 = _rms(x1, gf_ref[...])
    h2b = h2.astype(_bf16)
    z = jnp.dot(h2b, wr_ref[...], preferred_element_type=_f32) + br_ref[...]
    experts, weights = _route(z)
    re_ref[...] = experts
    rw_ref[...] = weights
    bits = pltpu.bitcast(h2b.astype(_f32), jnp.uint32)
    half = D_MODEL // 2
    _store_row_tiles(hp_ref, bits[:, :half] | (bits[:, half:] >> 16))


PACK_TILES = D_MODEL // 2 // LANES
ROW_TILES = D_MODEL // LANES


def _merge(n, oa, obs, lses, x, res_lse, ga, gb, wo_bf16, gf, wr_bf16, br):
    nb = len(obs)
    assert len(lses) == (nb if nb > 1 else 0)
    row = lambda r, w: pl.BlockSpec((r, w), lambda i: (i, 0))
    full = lambda a: pl.BlockSpec(a.shape, lambda i: (0,) * a.ndim)
    consts = [ga, gb, wo_bf16, gf, wr_bf16, br]
    tiled = [oa] + list(obs) + list(lses) + [x]
    return pl.pallas_call(
        functools.partial(_merge_kernel, n_branch=nb, res_lse=tuple(res_lse)),
        grid=(n // ROW_TM,),
        in_specs=[s for _, s in tiled] + [full(a) for a in consts],
        out_specs=[row(ROW_TM, D_MODEL), row(ROW_TM * PACK_TILES, LANES), row(ROW_TM, LANES), row(ROW_TM, LANES)],
        out_shape=[jax.ShapeDtypeStruct((n, D_MODEL), _f32),
                   jax.ShapeDtypeStruct((n * PACK_TILES, LANES), jnp.uint32),
                   jax.ShapeDtypeStruct((n, LANES), jnp.int32),
                   jax.ShapeDtypeStruct((n, LANES), _f32)],
        compiler_params=_cparams(("arbitrary",)),
        name=f"merge_{nb}",
    )(*[a for a, _ in tiled], *consts)


def _rank_kernel(re_ref, rank_ref, cnt_ref, carry):
    @pl.when(pl.program_id(0) == 0)
    def _():
        carry[...] = jnp.zeros_like(carry)

    e = re_ref[...]
    lane = _lane_iota(e.shape)
    r_i = lax.broadcasted_iota(jnp.int32, (ROW_TM, ROW_TM), 0)
    c_i = lax.broadcasted_iota(jnp.int32, (ROW_TM, ROW_TM), 1)
    before = (c_i < r_i).astype(_bf16)
    base = carry[...]
    ranks = []
    for k in range(2):
        onehot = lane == e[:, k:k + 1]
        oh = onehot.astype(_f32)
        earlier = jnp.dot(before, onehot.astype(_bf16), preferred_element_type=_f32)
        ranks.append(jnp.sum(oh * (earlier + base), axis=-1, keepdims=True))
        base = base + jnp.sum(oh, axis=0, keepdims=True)
    carry[...] = base
    cnt_ref[...] = base
    rank_ref[...] = jnp.where(lane == 0, ranks[0], jnp.where(lane == 1, ranks[1], 0.0)).astype(jnp.int32)


def _rank(route_e):
    n = route_e.shape[0]
    return pl.pallas_call(
        _rank_kernel,
        grid=(n // ROW_TM,),
        in_specs=[pl.BlockSpec((ROW_TM, LANES), lambda i: (i, 0))],
        out_specs=[pl.BlockSpec((ROW_TM, LANES), lambda i: (i, 0)), pl.BlockSpec((1, LANES), lambda i: (0, 0))],
        out_shape=[jax.ShapeDtypeStruct((n, LANES), jnp.int32), jax.ShapeDtypeStruct((1, LANES), _f32)],
        scratch_shapes=[pltpu.VMEM((1, LANES), _f32)],
        compiler_params=_cparams(("arbitrary",)),
        name="moe_rank",
    )(route_e)


PAIRS_PER_TILE = 2 * ROW_TM


def _row_copy(src, dst, i_src, i_dst, per, sem):
    return pltpu.make_async_copy(src.at[pl.ds(pl.multiple_of(i_src * per, per), per), :],
                                 dst.at[pl.ds(pl.multiple_of(i_dst * per, per), per), :], sem)


def _dispatch_kernel(dest_ref, src_ref, _init_ref, dst_ref, sem):
    def issue(p, c):
        _row_copy(src_ref, dst_ref, p // 2, dest_ref[p], PACK_TILES, sem).start()
        return c

    lax.fori_loop(0, PAIRS_PER_TILE, issue, 0, unroll=8)
    rows = ROW_TM * PACK_TILES
    for _ in range(2):
        pltpu.make_async_copy(src_ref, dst_ref.at[pl.ds(0, rows), :], sem).wait()


def _dispatch(dest_flat, src, out):
    anyspec = pl.BlockSpec(memory_space=pl.ANY)
    return pl.pallas_call(
        _dispatch_kernel,
        grid=(dest_flat.shape[0] // PAIRS_PER_TILE,),
        in_specs=[pl.BlockSpec((PAIRS_PER_TILE,), lambda i: (i,), memory_space=pltpu.SMEM),
                  pl.BlockSpec((ROW_TM * PACK_TILES, LANES), lambda i: (i, 0)), anyspec],
        out_specs=anyspec,
        out_shape=jax.ShapeDtypeStruct(out.shape, out.dtype),
        input_output_aliases={2: 0},
        scratch_shapes=[pltpu.SemaphoreType.DMA],
        compiler_params=pltpu.CompilerParams(dimension_semantics=("arbitrary",), has_side_effects=True,
                                             vmem_limit_bytes=VMEM_LIMIT),
        name="moe_dispatch",
    )(dest_flat, src, out)


def _ffn_kernel(te_ref, nt_ref, x_ref, wg_ref, wu_ref, wd_ref, y_ref):
    i = pl.program_id(0)

    @pl.when(i < nt_ref[0])
    def _():
        packed = _load_row_tiles(x_ref, MOE_TM)
        hi = [pltpu.bitcast(p & jnp.uint32(0xFFFF0000), _f32) for p in packed]
        lo = [pltpu.bitcast(p << 16, _f32) for p in packed]
        x = jnp.concatenate(hi + lo, axis=1).astype(_bf16)
        g = jnp.dot(x, wg_ref[...], preferred_element_type=_f32)
        u = jnp.dot(x, wu_ref[...], preferred_element_type=_f32)
        h = (jax.nn.silu(g) * u).astype(_bf16)
        _store_row_tiles(y_ref, jnp.dot(h, wd_ref[...], preferred_element_type=_f32))

    @pl.when(i >= nt_ref[0])
    def _():
        y_ref[...] = jnp.zeros_like(y_ref)


def _expert_ffn(tile_expert, n_tiles_used, x_disp, wg, wu, wd):
    rows = x_disp.shape[0] // PACK_TILES
    return pl.pallas_call(
        _ffn_kernel,
        grid_spec=pltpu.PrefetchScalarGridSpec(
            num_scalar_prefetch=2,
            grid=(rows // MOE_TM,),
            in_specs=[
                pl.BlockSpec((MOE_TM * PACK_TILES, LANES), lambda i, te, nt: (i, 0)),
                pl.BlockSpec((None, D_MODEL, D_EXPERT), lambda i, te, nt: (te[i], 0, 0)),
                pl.BlockSpec((None, D_MODEL, D_EXPERT), lambda i, te, nt: (te[i], 0, 0)),
                pl.BlockSpec((None, D_EXPERT, D_MODEL), lambda i, te, nt: (te[i], 0, 0)),
            ],
            out_specs=pl.BlockSpec((MOE_TM * ROW_TILES, LANES), lambda i, te, nt: (i, 0)),
        ),
        out_shape=jax.ShapeDtypeStruct((rows * ROW_TILES, LANES), _f32),
        compiler_params=_cparams(("arbitrary",)),
        name="moe_ffn",
    )(tile_expert, n_tiles_used, x_disp, wg, wu, wd)


def _final_kernel(dcur_ref, dnext_ref, x1_ref, rw_ref, g_ref, y_hbm, o_ref, ybuf, sems, *, n_tiles):
    i = pl.program_id(0)
    slot = i % 2

    def copy(dref, p, s):
        return _row_copy(y_hbm, ybuf.at[s, p % 2], dref[p], p // 2, ROW_TILES, sems.at[s])

    def issue(dref, s):
        def body(p, c):
            copy(dref, p, s).start()
            return c
        lax.fori_loop(0, PAIRS_PER_TILE, body, 0, unroll=8)

    @pl.when(i == 0)
    def _():
        issue(dcur_ref, 0)

    @pl.when(i + 1 < n_tiles)
    def _():
        issue(dnext_ref, 1 - slot)

    for k in range(2):
        pltpu.make_async_copy(y_hbm.at[pl.ds(0, ROW_TM * ROW_TILES), :], ybuf.at[slot, k], sems.at[slot]).wait()
    rw = rw_ref[...]
    y0 = jnp.concatenate(_load_row_tiles(ybuf.at[slot, 0], ROW_TM), axis=1)
    y1 = jnp.concatenate(_load_row_tiles(ybuf.at[slot, 1], ROW_TM), axis=1)
    moe = y0 * rw[:, 0:1] + y1 * rw[:, 1:2]
    out = _rms(x1_ref[...] + moe, g_ref[...])
    if len(o_ref.shape) == 2:
        o_ref[...] = out
    else:
        per = ROW_TM // RES
        for c in range(RES):
            o_ref[:, _res_of(c), :] = out[c * per:(c + 1) * per]


def _final(dest_flat, x1, rw, g, y_disp, residue_major):
    n = x1.shape[0]
    nt = n // ROW_TM
    dspec = lambda f: pl.BlockSpec((PAIRS_PER_TILE,), f, memory_space=pltpu.SMEM)
    if residue_major:
        out_spec = pl.BlockSpec((ROW_TM // RES, RES, D_MODEL), lambda i: (i, 0, 0))
        out_shape = jax.ShapeDtypeStruct((n // RES, RES, D_MODEL), _f32)
    else:
        out_spec = pl.BlockSpec((ROW_TM, D_MODEL), lambda i: (i, 0))
        out_shape = jax.ShapeDtypeStruct((n, D_MODEL), _f32)
    return pl.pallas_call(
        functools.partial(_final_kernel, n_tiles=nt),
        grid=(nt,),
        in_specs=[
            dspec(lambda i: (i,)),
            dspec(lambda i: (jnp.minimum(i + 1, nt - 1),)),
            pl.BlockSpec((ROW_TM, D_MODEL), lambda i: (i, 0)),
            pl.BlockSpec((ROW_TM, LANES), lambda i: (i, 0)),
            pl.BlockSpec((1, D_MODEL), lambda i: (0, 0)),
            pl.BlockSpec(memory_space=pl.ANY),
        ],
        out_specs=out_spec,
        out_shape=out_shape,
        scratch_shapes=[pltpu.VMEM((2, 2, ROW_TM * ROW_TILES, LANES), _f32), pltpu.SemaphoreType.DMA((2,))],
        compiler_params=_cparams(("arbitrary",)),
        name="moe_combine_final",
    )(dest_flat, dest_flat, x1, rw, g, y_disp)


def _moe_plan(route_e, rank, counts):
    counts = counts[0, :N_EXPERTS].astype(jnp.int32)
    padded = (counts + MOE_TM - 1) // MOE_TM * MOE_TM
    pend = jnp.cumsum(padded)
    pstart = pend - padded
    e = route_e[:, :2]
    dest = (pstart[e] + rank[:, :2]).reshape(-1)
    n_pairs = dest.shape[0]
    n_tiles = (n_pairs + N_EXPERTS * (MOE_TM - 1) + MOE_TM - 1) // MOE_TM
    tile_start = jnp.arange(n_tiles, dtype=jnp.int32) * MOE_TM
    tile_expert = jnp.minimum(jnp.sum(pend[None, :] <= tile_start[:, None], axis=1), N_EXPERTS - 1).astype(jnp.int32)
    n_used = (pend[-1] // MOE_TM).astype(jnp.int32).reshape(1)
    return dest, tile_expert, n_used, n_tiles


def kernel(x_prompt, x_sample, cache_a_k, cache_a_v, cache_b_k, cache_b_v, g_attn, w_in, attn_sinks, g_out_a,
           g_out_b, w_out, g_ffn, w_router_group, b_router_group, w_router_expert, b_router_expert, w_gate, w_up,
           w_down, g_final):
    depth = g_attn.shape[0]
    assert depth == 1
    b, t, _ = x_prompt.shape
    bd, s_len, _ = x_sample.shape
    assert t % (RES * BLOCK) == 0 and t % PROJ_TM == 0 and 8 % s_len == 0
    n_p, n_s = b * t, bd * s_len
    n_tok = n_p + n_s
    l = 0

    w_l = w_in[l]
    w_in_b = jnp.concatenate([w_l[:, :WIDTH_A], w_l[:, WIDTH_A + 2 * KV_WIDTH_A:],
                              w_l[:, WIDTH_A:WIDTH_A + 2 * KV_WIDTH_A]], axis=1).astype(_bf16)
    w_out_b = w_out[l].astype(_bf16)
    wg_b, wu_b, wd_b = w_gate[l].astype(_bf16), w_up[l].astype(_bf16), w_down[l].astype(_bf16)
    pad = LANES - N_GROUPS - N_EXPERTS
    w_r = jnp.concatenate([w_router_group[l], w_router_expert[l], jnp.zeros((D_MODEL, pad), _f32)], axis=1)
    b_r = jnp.concatenate([b_router_group[l], b_router_expert[l], jnp.zeros((pad,), _f32)])[None]
    g_a, g_oa, g_ob, g_f = g_attn[l][None], g_out_a[l][None], g_out_b[l][None], g_ffn[l][None]
    sinks = attn_sinks[l]

    cos_p, sin_p = _rope_tables(jnp.arange(t))
    p_nat, kv_f32, p_res = _project_prompt(x_prompt, g_a, w_in_b, cos_p, sin_p)
    (oa,) = _band_attention(p_nat[:, None], 1, t, (BLOCK,), (0, COL_KA // LANES, COL_VA // LANES), il=1, sinks=sinks)
    cb = (COL_QB // WIDTH_B, COL_KB // WIDTH_B, COL_VB // WIDTH_B)
    o1, l1 = _band_attention(p_nat[:, None], 1, t, (BLOCK,), cb, il=1)
    o4, l4 = _band_attention(p_res.reshape(b, 4, 4, t // RES, 3 * WIDTH_B), 4, t // 4, (4, BLOCK // 4), (0, 1, 2), il=4)
    o16, l16 = _band_attention(p_res, RES, t // RES, (BLOCK,), (0, 1, 2), il=1)

    upt = ROW_TM // RES
    tpb = t // ROW_TM
    sub4 = BLOCK // 4 // upt
    nat = lambda a, w: (a.reshape(n_p // RES, RES, w), pl.BlockSpec((upt, RES, w), lambda i: (i, 0, 0)))
    res4 = lambda a: (a.reshape(b, 4, t // RES // (BLOCK // 4), 4, BLOCK // 4, WIDTH_B),
                      pl.BlockSpec((None, 4, None, 4, upt, WIDTH_B),
                                   lambda i: (i // tpb, 0, (i % tpb) // sub4, 0, (i % tpb) % sub4, 0)))
    res16 = lambda a: (a, pl.BlockSpec((None, RES, upt, WIDTH_B), lambda i: (i // tpb, 0, i % tpb, 0)))
    obs_p = [nat(o1, WIDTH_B), res4(o4), res16(o16)]
    lses_p = [nat(l1, WIDTH_B), res4(l4), res16(l16)]

    cos_s, sin_s = _rope_tables(PAST_LEN + jnp.arange(n_s) % s_len)
    ps = _project_sample(x_sample.reshape(n_s, D_MODEL), g_a, w_in_b, cos_s, sin_s)
    to_t = lambda c: jnp.transpose(c[l], (0, 2, 3, 1))
    from_t = lambda c: jnp.transpose(c, (0, 3, 1, 2))[None]
    oa_s, ak_s, av_s = _sample_attention_a(ps, to_t(cache_a_k), to_t(cache_a_v), sinks, s_len)
    ob_s, bk_s, bv_s = _sample_attention_b(ps, to_t(cache_b_k), to_t(cache_b_v), s_len)

    consts = (g_oa, g_ob, w_out_b, g_f, w_r.astype(_bf16), b_r)
    x1_p, h2_p, re_p, rw_p = _merge(n_p, nat(oa, WIDTH_A), obs_p, lses_p, nat(x_prompt, D_MODEL),
                                    (False, True, True), *consts)
    plain = lambda a: (a, pl.BlockSpec((ROW_TM, a.shape[1]), lambda i: (i, 0)))
    x1_s, h2_s, re_s, rw_s = _merge(n_s, plain(oa_s), [plain(ob_s)], [], plain(x_sample.reshape(n_s, D_MODEL)),
                                    (False,), *consts)

    route_e = jnp.concatenate([re_p, re_s], axis=0)
    rank, counts = _rank(route_e)
    dest, tile_expert, n_used, n_tiles = _moe_plan(route_e, rank, counts)
    dest_p, dest_s = dest[:2 * n_p], dest[2 * n_p:]
    x_disp = jnp.zeros((n_tiles * MOE_TM * PACK_TILES, LANES), jnp.uint32)
    x_disp = _dispatch(dest_p, h2_p, x_disp)
    x_disp = _dispatch(dest_s, h2_s, x_disp)
    y_disp = _expert_ffn(tile_expert, n_used, x_disp, wg_b, wu_b, wd_b)
    g_fin = g_final[None]
    y_prompt = _final(dest_p, x1_p, rw_p, g_fin, y_disp, True).reshape(b, t, D_MODEL)
    y_sample = _final(dest_s, x1_s, rw_s, g_fin, y_disp, False).reshape(bd, s_len, D_MODEL)

    kv = kv_f32
    heads = lambda a, h: a.reshape(b, -1, h, HEAD_DIM)[None]
    bk_p = heads(kv[:, t - WINDOW_B:, 0:WIDTH_B], N_HEADS_B)
    bv_p = heads(kv[:, t - WINDOW_B:, WIDTH_B:2 * WIDTH_B], N_HEADS_B)
    ak_p = heads(kv[:, t - WINDOW_A:, 2 * WIDTH_B:2 * WIDTH_B + KV_WIDTH_A], N_KV_A)
    av_p = heads(kv[:, t - WINDOW_A:, 2 * WIDTH_B + KV_WIDTH_A:], N_KV_A)
    return (y_prompt, y_sample, ak_p, av_p, bk_p, bv_p, from_t(ak_s), from_t(av_s), from_t(bk_s), from_t(bv_s))
```

```python
import functools

import jax
import jax.numpy as jnp
from jax import lax
from jax.experimental import pallas as pl
from jax.experimental.pallas import tpu as pltpu

D_MODEL = 2048
HEAD_DIM = 64
N_HEADS_A = 16
N_KV_A = 2
GROUP_A = N_HEADS_A // N_KV_A
N_HEADS_B = 16
WIDTH_A = N_HEADS_A * HEAD_DIM
WIDTH_B = N_HEADS_B * HEAD_DIM
KV_WIDTH_A = N_KV_A * HEAD_DIM
PROJ_WIDTH = WIDTH_A + 2 * KV_WIDTH_A + 3 * WIDTH_B
WINDOW_A = 128
DILATED_PAIRS = ((128, 1), (512, 4), (2048, 16))
WINDOW_B = 2048
BLOCK = 128
ROPE_THETA = 10000.0
N_GROUPS = 4
EXPERTS_PER_GROUP = 8
N_EXPERTS = N_GROUPS * EXPERTS_PER_GROUP
D_EXPERT = D_MODEL // 2
RMS_EPS = 1e-6
NEG_INF = -1e30
PAST_LEN = 16384

LANES = 128
VMEM_LIMIT = 56 * 1024 * 1024

COL_QA = 0
COL_QB = WIDTH_A
COL_KB = COL_QB + WIDTH_B
COL_VB = COL_KB + WIDTH_B
COL_KA = COL_VB + WIDTH_B
COL_VA = COL_KA + KV_WIDTH_A
KV_COLS = 2 * WIDTH_B + 2 * KV_WIDTH_A

PROJ_TM = 512
PROJ_TN = 256
RES = 16
ROW_TM = 256
MOE_TM = 256

_f32 = jnp.float32
_bf16 = jnp.bfloat16


def _cparams(sem):
    return pltpu.CompilerParams(dimension_semantics=sem, vmem_limit_bytes=VMEM_LIMIT)


def _lane_iota(shape):
    return lax.broadcasted_iota(jnp.int32, shape, len(shape) - 1)


def _rope_tables(pos):
    half = HEAD_DIM // 2
    inv_freq = ROPE_THETA ** (-jnp.arange(half, dtype=_f32) / half)
    ang = pos.astype(_f32)[:, None] * inv_freq[None, :]
    cos, sin = jnp.cos(ang), jnp.sin(ang)
    return jnp.tile(cos, (1, 4)), jnp.tile(jnp.concatenate([-sin, sin], axis=-1), (1, 2))


def _proj_kernel(x_ref, g_ref, w_ref, cos_ref, sin_ref, *refs, prompt):
    if prompt:
        p_ref, kv_ref, res_ref, h_scr, mm_scr, rope_scr = refs
    else:
        p_ref, h_scr, mm_scr = refs
    j = pl.program_id(2)
    jj = j - 1

    @pl.when(j == 0)
    def _():
        x = x_ref[...]
        ms = jnp.mean(x * x, axis=-1, keepdims=True)
        h_scr[...] = (x * lax.rsqrt(ms + RMS_EPS) * g_ref[...]).astype(_bf16)
        mm_scr[...] = jnp.zeros_like(mm_scr)

    lane = _lane_iota((1, LANES))
    first_half = (lane % HEAD_DIM) < (HEAD_DIM // 2)
    cos, sin = cos_ref[...], sin_ref[...]
    for half in range(PROJ_TN // LANES):
        cs = slice(half * LANES, (half + 1) * LANES)
        sb = jj * (PROJ_TN // LANES) + half
        is_v = ((sb >= COL_VB // LANES) & (sb < COL_KA // LANES)) | (sb >= COL_VA // LANES)
        a = mm_scr[:, cs]
        swapped = jnp.where(first_half, pltpu.roll(a, LANES - HEAD_DIM // 2, axis=1),
                            pltpu.roll(a, HEAD_DIM // 2, axis=1))
        y = a * jnp.where(is_v, 1.0, cos) + swapped * jnp.where(is_v, 0.0, sin)
        if prompt:
            p_ref[:, cs] = y.astype(_bf16)
            kv_ref[:, cs] = y
            rope_scr[half] = y
        else:
            p_ref[:, cs] = y
    mm_scr[...] = jnp.dot(h_scr[...], w_ref[...], preferred_element_type=_f32)
    if not prompt:
        return

    @pl.when((jj >= COL_QB // PROJ_TN) & (jj < COL_KA // PROJ_TN))
    def _():
        rows = PROJ_TM // RES
        for c in range(RES):
            c16 = 4 * (c % 4) + c // 4
            for half in range(PROJ_TN // LANES):
                res_ref[c, :, half * LANES:(half + 1) * LANES] = (
                    rope_scr[half, pl.ds(c16, rows, stride=RES), :].astype(_bf16))


PROJ_NJ = PROJ_WIDTH // PROJ_TN


def _proj_in_specs(x_spec):
    return [
        x_spec,
        pl.BlockSpec((1, D_MODEL), lambda bi, i, j: (0, 0)),
        pl.BlockSpec((D_MODEL, PROJ_TN), lambda bi, i, j: (0, jnp.minimum(j, PROJ_NJ - 1))),
        pl.BlockSpec((PROJ_TM, LANES), lambda bi, i, j: (i, 0)),
        pl.BlockSpec((PROJ_TM, LANES), lambda bi, i, j: (i, 0)),
    ]


def _project_prompt(x, g, w_bf16, cos, sin):
    b, t, _ = x.shape
    jkb, jqb = COL_KB // PROJ_TN, COL_QB // PROJ_TN
    n_res = 3 * WIDTH_B // PROJ_TN
    p_map = lambda bi, i, j: (bi, i, jnp.maximum(j - 1, 0))
    kv_map = lambda bi, i, j: (bi, i, jnp.maximum(j - 1 - jkb, 0))
    res_map = lambda bi, i, j: (bi, 0, i, jnp.clip(j - 1 - jqb, 0, n_res - 1))
    return pl.pallas_call(
        functools.partial(_proj_kernel, prompt=True),
        grid=(b, t // PROJ_TM, PROJ_NJ + 1),
        in_specs=_proj_in_specs(pl.BlockSpec((None, PROJ_TM, D_MODEL), lambda bi, i, j: (bi, i, 0))),
        out_specs=[
            pl.BlockSpec((None, PROJ_TM, PROJ_TN), p_map),
            pl.BlockSpec((None, PROJ_TM, PROJ_TN), kv_map),
            pl.BlockSpec((None, RES, PROJ_TM // RES, PROJ_TN), res_map),
        ],
        out_shape=[
            jax.ShapeDtypeStruct((b, t, PROJ_WIDTH), _bf16),
            jax.ShapeDtypeStruct((b, t, KV_COLS), _f32),
            jax.ShapeDtypeStruct((b, RES, t // RES, 3 * WIDTH_B), _bf16),
        ],
        scratch_shapes=[pltpu.VMEM((PROJ_TM, D_MODEL), _bf16), pltpu.VMEM((PROJ_TM, PROJ_TN), _f32),
                        pltpu.VMEM((PROJ_TN // LANES, PROJ_TM, LANES), _f32)],
        compiler_params=_cparams(("arbitrary", "arbitrary", "arbitrary")),
        name="proj_prompt",
    )(x, g, w_bf16, cos, sin)


def _project_sample(x, g, w_bf16, cos, sin):
    n = x.shape[0]
    return pl.pallas_call(
        functools.partial(_proj_kernel, prompt=False),
        grid=(1, n // PROJ_TM, PROJ_NJ + 1),
        in_specs=_proj_in_specs(pl.BlockSpec((PROJ_TM, D_MODEL), lambda bi, i, j: (i, 0))),
        out_specs=pl.BlockSpec((PROJ_TM, PROJ_TN), lambda bi, i, j: (i, jnp.maximum(j - 1, 0))),
        out_shape=jax.ShapeDtypeStruct((n, PROJ_WIDTH), _f32),
        scratch_shapes=[pltpu.VMEM((PROJ_TM, D_MODEL), _bf16), pltpu.VMEM((PROJ_TM, PROJ_TN), _f32)],
        compiler_params=_cparams(("arbitrary", "arbitrary", "arbitrary")),
        name="proj_sample",
    )(x, g, w_bf16, cos, sin)


def _half_masks():
    lane = _lane_iota((1, LANES))
    return lane < HEAD_DIM


def _dup_half(x, g):
    lo = _half_masks()
    r = pltpu.roll(x, HEAD_DIM, axis=x.ndim - 1)
    return jnp.where(lo, x, r) if g == 0 else jnp.where(lo, r, x)


def _band_attn_kernel(*refs, il, gqa):
    if gqa:
        sink_ref, q_ref, kp_ref, kc_ref, vp_ref, vc_ref, o_ref = refs
    else:
        q_ref, kp_ref, kc_ref, vp_ref, vc_ref, o_ref, lse_ref = refs
    blk = pl.program_id(2)
    q = q_ref[...].reshape(BLOCK, -1)
    k2 = jnp.concatenate([kp_ref[...].reshape(BLOCK, -1), kc_ref[...].reshape(BLOCK, -1)], axis=0)
    v2 = jnp.concatenate([vp_ref[...].reshape(BLOCK, -1), vc_ref[...].reshape(BLOCK, -1)], axis=0)

    per = BLOCK // il
    w = lambda idx: il * (idx % per) + idx // per
    wi = w(lax.broadcasted_iota(jnp.int32, (2 * BLOCK, BLOCK), 0) % BLOCK)
    wj = w(lax.broadcasted_iota(jnp.int32, (2 * BLOCK, BLOCK), 1))
    newer = wj > wi
    same = wj == wi
    prev_bias = jnp.where(blk > 0, 0.0, NEG_INF)

    lo = _half_masks()
    zero = jnp.zeros((), q.dtype)
    q = q * jnp.asarray(HEAD_DIM ** -0.5, q.dtype)
    if gqa:
        kf, vf = k2.astype(_f32), v2.astype(_f32)
        kdup = [_dup_half(kf, g).astype(_bf16) for g in range(N_KV_A)]
        vdup = [_dup_half(vf, g).astype(_bf16) for g in range(N_KV_A)]
    for hp in range(q.shape[1] // LANES):
        qp = q[:, hp * LANES:(hp + 1) * LANES]
        if gqa:
            kk = kdup[(2 * hp) // GROUP_A]
            vv = vdup[(2 * hp) // GROUP_A]
        else:
            kk = k2[:, hp * LANES:(hp + 1) * LANES]
            vv = v2[:, hp * LANES:(hp + 1) * LANES]
        qs = jnp.concatenate([jnp.where(lo, qp, zero), jnp.where(lo, zero, qp)], axis=0)
        s = lax.dot_general(qs, kk, (((1,), (1,)), ((), ())), preferred_element_type=_f32)
        s_prev = s[:, :BLOCK] + prev_bias
        s_band = jnp.where(newer, s_prev, s[:, BLOCK:])
        s_edge = jnp.sum(jnp.where(same, s_prev, 0.0), axis=-1, keepdims=True)
        m = jnp.maximum(jnp.max(s_band, axis=-1, keepdims=True), s_edge)
        e = jnp.exp(s_band - m)
        e_edge = jnp.exp(s_edge - m)
        den = jnp.sum(e, axis=-1, keepdims=True) + e_edge
        p_prev = jnp.where(newer, e, jnp.where(same, e_edge, 0.0))
        p_cur = jnp.where(newer, 0.0, e)
        p = jnp.concatenate([p_prev, p_cur], axis=1).astype(_bf16)
        o2 = jnp.dot(p, vv, preferred_element_type=_f32) / den
        lse = m + jnp.log(den)
        if gqa:
            o2 = jnp.concatenate([
                o2[:BLOCK] * jax.nn.sigmoid(lse[:BLOCK] - sink_ref[2 * hp]),
                o2[BLOCK:] * jax.nn.sigmoid(lse[BLOCK:] - sink_ref[2 * hp + 1])], axis=0)
        o_ref[:, hp * LANES:(hp + 1) * LANES] = jnp.where(lo, o2[:BLOCK], o2[BLOCK:])
        if not gqa:
            lse_ref[:, hp * LANES:(hp + 1) * LANES] = jnp.where(
                lo, jnp.broadcast_to(lse[:BLOCK], (BLOCK, LANES)), jnp.broadcast_to(lse[BLOCK:], (BLOCK, LANES)))


def _band_attention(arr, n_seq, seq_len, row_block, col_blocks, *, il, sinks=None):
    gqa = sinks is not None
    b = arr.shape[0]
    nblk = seq_len // BLOCK
    wq = WIDTH_A if gqa else WIDTH_B
    wk = LANES if gqa else WIDTH_B
    cq, ck, cv = col_blocks
    assert len(arr.shape) == 3 + len(row_block)

    def spec(width, col, prev):
        def imap(bi, si, blk):
            r = jnp.maximum(blk - 1, 0) if prev else blk
            return (bi, si) + ((0, r) if len(row_block) == 2 else (r,)) + (col,)
        return pl.BlockSpec((None, None) + tuple(row_block) + (width,), imap)

    in_specs = [spec(wq, cq, False), spec(wk, ck, True), spec(wk, ck, False), spec(wk, cv, True), spec(wk, cv, False)]
    args = [arr] * 5
    if gqa:
        in_specs = [pl.BlockSpec(memory_space=pltpu.SMEM)] + in_specs
        args = [sinks] + args
    out_map = lambda bi, si, blk: (bi, si, blk, 0)
    n_out = 1 if gqa else 2
    outs = pl.pallas_call(
        functools.partial(_band_attn_kernel, il=il, gqa=gqa),
        grid=(b, n_seq, nblk),
        in_specs=in_specs,
        out_specs=[pl.BlockSpec((None, None, BLOCK, wq), out_map)] * n_out,
        out_shape=[jax.ShapeDtypeStruct((b, n_seq, seq_len, wq), _f32)] * n_out,
        compiler_params=_cparams(("arbitrary", "arbitrary", "arbitrary")),
        name="band_attn_a" if gqa else f"band_attn_b_il{il}_{n_seq}",
    )(*args)
    return outs


def _pad_rows(x, rows):
    return jnp.concatenate([x, jnp.zeros((rows - x.shape[0], x.shape[1]), x.dtype)], axis=0)


def _shift_in(cache, new_t, steps):
    lane = _lane_iota((1, LANES))
    keep = lane < LANES - steps
    ncol = cache.shape[1] // LANES
    cols = []
    prev = pltpu.roll(cache[:, :LANES], LANES - steps, axis=1)
    for c in range(ncol):
        nxt = new_t if c == ncol - 1 else pltpu.roll(cache[:, (c + 1) * LANES:(c + 2) * LANES], LANES - steps, axis=1)
        cols.append(jnp.where(keep, prev, nxt))
        prev = nxt
    return jnp.concatenate(cols, axis=1) if ncol > 1 else cols[0]


def _new_cols(new_rows, bi, s_len):
    t = jnp.transpose(_pad_rows(new_rows, LANES))
    return pltpu.roll(t, (LANES - s_len - bi * s_len) % LANES, axis=1)


def _sample_a_kernel(sink_ref, q_ref, kvn_ref, kc_ref, vc_ref, o_ref, ko_ref, vo_ref, *, s_len):
    nb = 8 // s_len
    q8 = q_ref[...]
    kn8 = kvn_ref[:, :LANES]
    vn8 = kvn_ref[:, LANES:]
    lo = _half_masks()
    row = lax.broadcasted_iota(jnp.int32, (8, 1), 0)
    scale = HEAD_DIM ** -0.5
    npair = GROUP_A // 2
    rows_g = npair * 16
    qrow = lax.broadcasted_iota(jnp.int32, (rows_g, LANES), 0) % 8
    lane = _lane_iota((rows_g, LANES))
    qb, qs_ = qrow // s_len, qrow % s_len
    out = jnp.zeros((8, WIDTH_A), _f32)
    for bi in range(nb):
        mine = (row // s_len) == bi
        kst = kc_ref[bi].reshape(N_KV_A * HEAD_DIM, WINDOW_A)
        vst = vc_ref[bi].reshape(N_KV_A * HEAD_DIM, WINDOW_A)
        ko_ref[bi] = _shift_in(kst, _new_cols(kn8, bi, s_len), s_len).reshape(N_KV_A, HEAD_DIM, WINDOW_A)
        vo_ref[bi] = _shift_in(vst, _new_cols(vn8, bi, s_len), s_len).reshape(N_KV_A, HEAD_DIM, WINDOW_A)
        kpos = PAST_LEN - WINDOW_A + lane
        d_c = PAST_LEN + qs_ - kpos
        mask_c = (d_c >= 0) & (d_c <= WINDOW_A) & (kpos >= 0)
        d_n = qs_ - lane % s_len
        mask_n = (lane < 8) & ((lane // s_len) == bi) & (d_n >= 0) & (d_n <= WINDOW_A)
        cols = []
        for g in range(N_KV_A):
            kdup = jnp.concatenate([kst[g * HEAD_DIM:(g + 1) * HEAD_DIM]] * 2, axis=0).astype(_bf16)
            vdup = jnp.concatenate([vst[g * HEAD_DIM:(g + 1) * HEAD_DIM]] * 2, axis=0).astype(_bf16)
            kn = _pad_rows(_dup_half(kn8, g), LANES).astype(_bf16)
            vn = _pad_rows(_dup_half(vn8, g), LANES).astype(_bf16)
            parts = []
            for pp in range(npair):
                hp = g * npair + pp
                qp = jnp.where(mine, q8[:, hp * LANES:(hp + 1) * LANES], 0.0)
                parts += [jnp.where(lo, qp, 0.0), jnp.where(lo, 0.0, qp)]
            qs = jnp.concatenate(parts, axis=0).astype(_bf16)
            s_c = jnp.dot(qs, kdup, preferred_element_type=_f32) * scale
            s_n = lax.dot_general(qs, kn, (((1,), (1,)), ((), ())), preferred_element_type=_f32) * scale
            s_c = jnp.where(mask_c, s_c, NEG_INF)
            s_n = jnp.where(mask_n, s_n, NEG_INF)
            m = jnp.maximum(jnp.max(s_c, axis=-1, keepdims=True), jnp.max(s_n, axis=-1, keepdims=True))
            p_c, p_n = jnp.exp(s_c - m), jnp.exp(s_n - m)
            den = jnp.sum(p_c, axis=-1, keepdims=True) + jnp.sum(p_n, axis=-1, keepdims=True)
            o = lax.dot_general(p_c.astype(_bf16), vdup, (((1,), (1,)), ((), ())), preferred_element_type=_f32)
            o = (o + jnp.dot(p_n.astype(_bf16), vn, preferred_element_type=_f32)) / den
            lse = m + jnp.log(den)
            for pp in range(npair):
                h0 = 2 * (g * npair + pp)
                r0 = pp * 16
                o0 = o[r0:r0 + 8] * jax.nn.sigmoid(lse[r0:r0 + 8] - sink_ref[h0])
                o1 = o[r0 + 8:r0 + 16] * jax.nn.sigmoid(lse[r0 + 8:r0 + 16] - sink_ref[h0 + 1])
                cols.append(jnp.where(lo, o0, o1))
        out = jnp.where(mine, jnp.concatenate(cols, axis=1), out)
    o_ref[...] = out


def _sample_attention_a(ps, kc_t, vc_t, sinks, s_len):
    n = ps.shape[0]
    nb = 8 // s_len
    bd = n // s_len
    cache_spec = pl.BlockSpec((nb, N_KV_A, HEAD_DIM, WINDOW_A), lambda i: (i, 0, 0, 0))
    return pl.pallas_call(
        functools.partial(_sample_a_kernel, s_len=s_len),
        grid=(n // 8,),
        in_specs=[
            pl.BlockSpec(memory_space=pltpu.SMEM),
            pl.BlockSpec((8, WIDTH_A), lambda i: (i, 0)),
            pl.BlockSpec((8, 2 * KV_WIDTH_A), lambda i: (i, COL_KA // (2 * KV_WIDTH_A))),
            cache_spec, cache_spec,
        ],
        out_specs=[pl.BlockSpec((8, WIDTH_A), lambda i: (i, 0)), cache_spec, cache_spec],
        out_shape=[
            jax.ShapeDtypeStruct((n, WIDTH_A), _f32),
            jax.ShapeDtypeStruct((bd, N_KV_A, HEAD_DIM, WINDOW_A), _f32),
            jax.ShapeDtypeStruct((bd, N_KV_A, HEAD_DIM, WINDOW_A), _f32),
        ],
        compiler_params=_cparams(("arbitrary",)),
        name="sample_attn_a",
    )(sinks, ps, ps, kc_t, vc_t)


SAMPLE_B_PAIRS = 2


def _sample_b_kernel(q_ref, kn_ref, vn_ref, kc_ref, vc_ref, o_ref, ko_ref, vo_ref, *, s_len):
    for g in range(SAMPLE_B_PAIRS):
        cs = slice(g * LANES, (g + 1) * LANES)
        hs = slice(2 * g, 2 * g + 2)
        _sample_b_pair(q_ref[:, cs], kn_ref[:, cs], vn_ref[:, cs], kc_ref, vc_ref, o_ref, ko_ref, vo_ref, cs, hs, s_len)


def _sample_b_pair(q8, kn8, vn8, kc_ref, vc_ref, o_ref, ko_ref, vo_ref, cs, hs, s_len):
    nb = 8 // s_len
    lo = _half_masks()
    row = lax.broadcasted_iota(jnp.int32, (8, 1), 0)
    scale = HEAD_DIM ** -0.5
    qrow = lax.broadcasted_iota(jnp.int32, (16, 1), 0) % 8
    qs_ = qrow % s_len
    kn = _pad_rows(kn8, LANES).astype(_bf16)
    vn = _pad_rows(vn8, LANES).astype(_bf16)
    lane_n = _lane_iota((16, LANES))
    out = jnp.zeros((8, LANES), _f32)
    for bi in range(nb):
        mine = (row // s_len) == bi
        kst = kc_ref[bi, hs].reshape(LANES, WINDOW_B)
        vst = vc_ref[bi, hs].reshape(LANES, WINDOW_B)
        ko_ref[bi, hs] = _shift_in(kst, _new_cols(kn8, bi, s_len), s_len).reshape(2, HEAD_DIM, WINDOW_B)
        vo_ref[bi, hs] = _shift_in(vst, _new_cols(vn8, bi, s_len), s_len).reshape(2, HEAD_DIM, WINDOW_B)
        qp = jnp.where(mine, q8, 0.0)
        qs = jnp.concatenate([jnp.where(lo, qp, 0.0), jnp.where(lo, 0.0, qp)], axis=0).astype(_bf16)
        s_c = jnp.dot(qs, kst.astype(_bf16), preferred_element_type=_f32) * scale
        s_n = lax.dot_general(qs, kn, (((1,), (1,)), ((), ())), preferred_element_type=_f32) * scale
        d_n = qs_ - lane_n % s_len
        ok_n = (lane_n < 8) & ((lane_n // s_len) == bi) & (d_n >= 0)
        branches = []
        for w, r in DILATED_PAIRS:
            lo_i = WINDOW_B - w
            lane_c = lo_i + _lane_iota((16, w))
            d_c = WINDOW_B + qs_ - lane_c
            ok_c = (d_c % r == 0) & (d_c <= w) & (PAST_LEN - WINDOW_B + lane_c >= 0)
            sc = jnp.where(ok_c, s_c[:, lo_i:], NEG_INF)
            sn = jnp.where(ok_n & (d_n % r == 0) & (d_n <= w), s_n, NEG_INF)
            m = jnp.maximum(jnp.max(sc, axis=-1, keepdims=True), jnp.max(sn, axis=-1, keepdims=True))
            branches.append((lo_i, sc, sn, m))
        m_all = functools.reduce(jnp.maximum, [br[3] for br in branches])
        p_c = jnp.zeros((16, WINDOW_B), _f32)
        p_n = jnp.zeros((16, LANES), _f32)
        for lo_i, sc, sn, m in branches:
            e = jnp.exp(sc - m_all)
            p_c = p_c + (jnp.concatenate([jnp.zeros((16, lo_i), _f32), e], axis=1) if lo_i else e)
            p_n = p_n + jnp.exp(sn - m_all)
        den = jnp.sum(p_c, axis=-1, keepdims=True) + jnp.sum(p_n, axis=-1, keepdims=True)
        o = lax.dot_general(p_c.astype(_bf16), vst.astype(_bf16), (((1,), (1,)), ((), ())), preferred_element_type=_f32)
        o = (o + jnp.dot(p_n.astype(_bf16), vn, preferred_element_type=_f32)) / den
        out = jnp.where(mine, jnp.where(lo, o[:8], o[8:]), out)
    o_ref[:, cs] = out


def _sample_attention_b(ps, kc_t, vc_t, s_len):
    n = ps.shape[0]
    nb = 8 // s_len
    bd = n // s_len
    wblk = SAMPLE_B_PAIRS * LANES
    nstep = WIDTH_B // wblk
    cache_spec = pl.BlockSpec((nb, 2 * SAMPLE_B_PAIRS, HEAD_DIM, WINDOW_B), lambda i, hp: (i, hp, 0, 0))
    col = lambda base: (lambda i, hp: (i, base // wblk + hp))
    return pl.pallas_call(
        functools.partial(_sample_b_kernel, s_len=s_len),
        grid=(n // 8, nstep),
        in_specs=[
            pl.BlockSpec((8, wblk), col(COL_QB)),
            pl.BlockSpec((8, wblk), col(COL_KB)),
            pl.BlockSpec((8, wblk), col(COL_VB)),
            cache_spec, cache_spec,
        ],
        out_specs=[pl.BlockSpec((8, wblk), lambda i, hp: (i, hp)), cache_spec, cache_spec],
        out_shape=[
            jax.ShapeDtypeStruct((n, WIDTH_B), _f32),
            jax.ShapeDtypeStruct((bd, N_HEADS_B, HEAD_DIM, WINDOW_B), _f32),
            jax.ShapeDtypeStruct((bd, N_HEADS_B, HEAD_DIM, WINDOW_B), _f32),
        ],
        compiler_params=_cparams(("arbitrary", "arbitrary")),
        name="sample_attn_b",
    )(ps, ps, ps, kc_t, vc_t)


def _rms(x, g):
    return x * lax.rsqrt(jnp.mean(x * x, axis=-1, keepdims=True) + RMS_EPS) * g


def _route(z):
    lane = _lane_iota(z.shape)
    big = jnp.int32(1 << 20)
    glane = lane < N_GROUPS
    gmax = jnp.max(jnp.where(glane, z, -jnp.inf), axis=-1, keepdims=True)
    grp = jnp.min(jnp.where(glane & (z == gmax), lane, big), axis=-1, keepdims=True)
    p_grp = 1.0 / jnp.sum(jnp.where(glane, jnp.exp(z - gmax), 0.0), axis=-1, keepdims=True)
    elane = (lane >= N_GROUPS) & (lane < N_GROUPS + N_EXPERTS) & ((lane - N_GROUPS) // EXPERTS_PER_GROUP == grp)
    t1 = jnp.max(jnp.where(elane, z, -jnp.inf), axis=-1, keepdims=True)
    i1 = jnp.min(jnp.where(elane & (z == t1), lane, big), axis=-1, keepdims=True)
    elane2 = elane & (lane != i1)
    t2 = jnp.max(jnp.where(elane2, z, -jnp.inf), axis=-1, keepdims=True)
    i2 = jnp.min(jnp.where(elane2 & (z == t2), lane, big), axis=-1, keepdims=True)
    e = jnp.exp(t2 - t1)
    w1 = p_grp * (1.0 / (1.0 + e))
    w2 = p_grp * (e / (1.0 + e))
    experts = jnp.where(lane == 0, i1 - N_GROUPS, jnp.where(lane == 1, i2 - N_GROUPS, 0))
    weights = jnp.where(lane == 0, w1, jnp.where(lane == 1, w2, 0.0))
    return experts, weights


def _store_row_tiles(ref, val):
    t, c = val.shape
    per = c // LANES
    for j in range(per):
        ref[pl.ds(j, t, stride=per), :] = val[:, j * LANES:(j + 1) * LANES]


def _load_row_tiles(ref, t):
    per = ref.shape[0] // t
    return [ref[pl.ds(j, t, stride=per), :] for j in range(per)]


def _res_of(c):
    return 4 * (c % 4) + c // 4


def _tile_rows(ref):
    if len(ref.shape) == 2:
        return ref[...]
    return jnp.concatenate([ref[:, _res_of(c), :] for c in range(RES)], axis=0)


def _merge_kernel(*refs, n_branch, res_lse):
    n_lse = n_branch if n_branch > 1 else 0
    oa_ref = refs[0]
    ob_refs = refs[1:1 + n_branch]
    lse_refs = refs[1 + n_branch:1 + n_branch + n_lse]
    x_ref, ga_ref, gb_ref, wo_ref, gf_ref, wr_ref, br_ref = refs[1 + n_branch + n_lse:8 + n_branch + n_lse]
    x1_ref, hp_ref, re_ref, rw_ref = refs[-4:]
    rows = lambda r, k: r[...].reshape(ROW_TM, r.shape[-1]) if res_lse[k] else _tile_rows(r)
    if n_branch == 1:
        ob = rows(ob_refs[0], 0)
    else:
        lses = [rows(r, k) for k, r in enumerate(lse_refs)]
        lmax = functools.reduce(jnp.maximum, lses)
        es = [jnp.exp(l - lmax) for l in lses]
        tot = functools.reduce(jnp.add, es)
        ob = functools.reduce(jnp.add, [(e / tot) * rows(r, k) for k, (e, r) in enumerate(zip(es, ob_refs))])
    o = jnp.concatenate([_rms(_tile_rows(oa_ref), ga_ref[...]), _rms(ob, gb_ref[...])], axis=-1).astype(_bf16)
    x1 = _tile_rows(x_ref) + jnp.dot(o, wo_ref[...], preferred_element_type=_f32)
    x1_ref[...] = x1
    h2 = _rms(x1, gf_ref[...])
    h2b = h2.astype(_bf16)
    z = jnp.dot(h2b, wr_ref[...], preferred_element_type=_f32) + br_ref[...]
    experts, weights = _route(z)
    re_ref[...] = experts
    rw_ref[...] = weights
    bits = pltpu.bitcast(h2b.astype(_f32), jnp.uint32)
    half = D_MODEL // 2
    _store_row_tiles(hp_ref, bits[:, :half] | (bits[:, half:] >> 16))


PACK_TILES = D_MODEL // 2 // LANES
ROW_TILES = D_MODEL // LANES


def _merge(n, oa, obs, lses, x, res_lse, ga, gb, wo_bf16, gf, wr_bf16, br):
    nb = len(obs)
    assert len(lses) == (nb if nb > 1 else 0)
    row = lambda r, w: pl.BlockSpec((r, w), lambda i: (i, 0))
    full = lambda a: pl.BlockSpec(a.shape, lambda i: (0,) * a.ndim)
    consts = [ga, gb, wo_bf16, gf, wr_bf16, br]
    tiled = [oa] + list(obs) + list(lses) + [x]
    return pl.pallas_call(
        functools.partial(_merge_kernel, n_branch=nb, res_lse=tuple(res_lse)),
        grid=(n // ROW_TM,),
        in_specs=[s for _, s in tiled] + [full(a) for a in consts],
        out_specs=[row(ROW_TM, D_MODEL), row(ROW_TM * PACK_TILES, LANES), row(ROW_TM, LANES), row(ROW_TM, LANES)],
        out_shape=[jax.ShapeDtypeStruct((n, D_MODEL), _f32),
                   jax.ShapeDtypeStruct((n * PACK_TILES, LANES), jnp.uint32),
                   jax.ShapeDtypeStruct((n, LANES), jnp.int32),
                   jax.ShapeDtypeStruct((n, LANES), _f32)],
        compiler_params=_cparams(("arbitrary",)),
        name=f"merge_{nb}",
    )(*[a for a, _ in tiled], *consts)


def _rank_kernel(re_ref, rank_ref, cnt_ref, carry):
    @pl.when(pl.program_id(0) == 0)
    def _():
        carry[...] = jnp.zeros_like(carry)

    e = re_ref[...]
    lane = _lane_iota(e.shape)
    r_i = lax.broadcasted_iota(jnp.int32, (ROW_TM, ROW_TM), 0)
    c_i = lax.broadcasted_iota(jnp.int32, (ROW_TM, ROW_TM), 1)
    before = (c_i < r_i).astype(_bf16)
    base = carry[...]
    ranks = []
    for k in range(2):
        onehot = lane == e[:, k:k + 1]
        oh = onehot.astype(_f32)
        earlier = jnp.dot(before, onehot.astype(_bf16), preferred_element_type=_f32)
        ranks.append(jnp.sum(oh * (earlier + base), axis=-1, keepdims=True))
        base = base + jnp.sum(oh, axis=0, keepdims=True)
    carry[...] = base
    cnt_ref[...] = base
    rank_ref[...] = jnp.where(lane == 0, ranks[0], jnp.where(lane == 1, ranks[1], 0.0)).astype(jnp.int32)


def _rank(route_e):
    n = route_e.shape[0]
    return pl.pallas_call(
        _rank_kernel,
        grid=(n // ROW_TM,),
        in_specs=[pl.BlockSpec((ROW_TM, LANES), lambda i: (i, 0))],
        out_specs=[pl.BlockSpec((ROW_TM, LANES), lambda i: (i, 0)), pl.BlockSpec((1, LANES), lambda i: (0, 0))],
        out_shape=[jax.ShapeDtypeStruct((n, LANES), jnp.int32), jax.ShapeDtypeStruct((1, LANES), _f32)],
        scratch_shapes=[pltpu.VMEM((1, LANES), _f32)],
        compiler_params=_cparams(("arbitrary",)),
        name="moe_rank",
    )(route_e)


PAIRS_PER_TILE = 2 * ROW_TM


def _row_copy(src, dst, i_src, i_dst, per, sem):
    return pltpu.make_async_copy(src.at[pl.ds(pl.multiple_of(i_src * per, per), per), :],
                                 dst.at[pl.ds(pl.multiple_of(i_dst * per, per), per), :], sem)


def _dispatch_kernel(dest_ref, src_ref, _init_ref, dst_ref, sem):
    def issue(p, c):
        _row_copy(src_ref, dst_ref, p // 2, dest_ref[p], PACK_TILES, sem).start()
        return c

    lax.fori_loop(0, PAIRS_PER_TILE, issue, 0, unroll=8)
    rows = ROW_TM * PACK_TILES
    for _ in range(2):
        pltpu.make_async_copy(src_ref, dst_ref.at[pl.ds(0, rows), :], sem).wait()


def _dispatch(dest_flat, src, out):
    anyspec = pl.BlockSpec(memory_space=pl.ANY)
    return pl.pallas_call(
        _dispatch_kernel,
        grid=(dest_flat.shape[0] // PAIRS_PER_TILE,),
        in_specs=[pl.BlockSpec((PAIRS_PER_TILE,), lambda i: (i,), memory_space=pltpu.SMEM),
                  pl.BlockSpec((ROW_TM * PACK_TILES, LANES), lambda i: (i, 0)), anyspec],
        out_specs=anyspec,
        out_shape=jax.ShapeDtypeStruct(out.shape, out.dtype),
        input_output_aliases={2: 0},
        scratch_shapes=[pltpu.SemaphoreType.DMA],
        compiler_params=pltpu.CompilerParams(dimension_semantics=("arbitrary",), has_side_effects=True,
                                             vmem_limit_bytes=VMEM_LIMIT),
        name="moe_dispatch",
    )(dest_flat, src, out)


def _ffn_kernel(te_ref, nt_ref, x_ref, wg_ref, wu_ref, wd_ref, y_ref):
    i = pl.program_id(0)

    @pl.when(i < nt_ref[0])
    def _():
        packed = _load_row_tiles(x_ref, MOE_TM)
        hi = [pltpu.bitcast(p & jnp.uint32(0xFFFF0000), _f32) for p in packed]
        lo = [pltpu.bitcast(p << 16, _f32) for p in packed]
        x = jnp.concatenate(hi + lo, axis=1).astype(_bf16)
        g = jnp.dot(x, wg_ref[...], preferred_element_type=_f32)
        u = jnp.dot(x, wu_ref[...], preferred_element_type=_f32)
        h = (jax.nn.silu(g) * u).astype(_bf16)
        _store_row_tiles(y_ref, jnp.dot(h, wd_ref[...], preferred_element_type=_f32))

    @pl.when(i >= nt_ref[0])
    def _():
        y_ref[...] = jnp.zeros_like(y_ref)


def _expert_ffn(tile_expert, n_tiles_used, x_disp, wg, wu, wd):
    rows = x_disp.shape[0] // PACK_TILES
    return pl.pallas_call(
        _ffn_kernel,
        grid_spec=pltpu.PrefetchScalarGridSpec(
            num_scalar_prefetch=2,
            grid=(rows // MOE_TM,),
            in_specs=[
                pl.BlockSpec((MOE_TM * PACK_TILES, LANES), lambda i, te, nt: (i, 0)),
                pl.BlockSpec((None, D_MODEL, D_EXPERT), lambda i, te, nt: (te[i], 0, 0)),
                pl.BlockSpec((None, D_MODEL, D_EXPERT), lambda i, te, nt: (te[i], 0, 0)),
                pl.BlockSpec((None, D_EXPERT, D_MODEL), lambda i, te, nt: (te[i], 0, 0)),
            ],
            out_specs=pl.BlockSpec((MOE_TM * ROW_TILES, LANES), lambda i, te, nt: (i, 0)),
        ),
        out_shape=jax.ShapeDtypeStruct((rows * ROW_TILES, LANES), _f32),
        compiler_params=_cparams(("arbitrary",)),
        name="moe_ffn",
    )(tile_expert, n_tiles_used, x_disp, wg, wu, wd)


def _final_kernel(dcur_ref, dnext_ref, x1_ref, rw_ref, g_ref, y_hbm, o_ref, ybuf, sems, *, n_tiles):
    i = pl.program_id(0)
    slot = i % 2

    def copy(dref, p, s):
        return _row_copy(y_hbm, ybuf.at[s, p % 2], dref[p], p // 2, ROW_TILES, sems.at[s])

    def issue(dref, s):
        def body(p, c):
            copy(dref, p, s).start()
            return c
        lax.fori_loop(0, PAIRS_PER_TILE, body, 0, unroll=8)

    @pl.when(i == 0)
    def _():
        issue(dcur_ref, 0)

    @pl.when(i + 1 < n_tiles)
    def _():
        issue(dnext_ref, 1 - slot)

    for k in range(2):
        pltpu.make_async_copy(y_hbm.at[pl.ds(0, ROW_TM * ROW_TILES), :], ybuf.at[slot, k], sems.at[slot]).wait()
    rw = rw_ref[...]
    y0 = jnp.concatenate(_load_row_tiles(ybuf.at[slot, 0], ROW_TM), axis=1)
    y1 = jnp.concatenate(_load_row_tiles(ybuf.at[slot, 1], ROW_TM), axis=1)
    moe = y0 * rw[:, 0:1] + y1 * rw[:, 1:2]
    out = _rms(x1_ref[...] + moe, g_ref[...])
    if len(o_ref.shape) == 2:
        o_ref[...] = out
    else:
        per = ROW_TM // RES
        for c in range(RES):
            o_ref[:, _res_of(c), :] = out[c * per:(c + 1) * per]


def _final(dest_flat, x1, rw, g, y_disp, residue_major):
    n = x1.shape[0]
    nt = n // ROW_TM
    dspec = lambda f: pl.BlockSpec((PAIRS_PER_TILE,), f, memory_space=pltpu.SMEM)
    if residue_major:
        out_spec = pl.BlockSpec((ROW_TM // RES, RES, D_MODEL), lambda i: (i, 0, 0))
        out_shape = jax.ShapeDtypeStruct((n // RES, RES, D_MODEL), _f32)
    else:
        out_spec = pl.BlockSpec((ROW_TM, D_MODEL), lambda i: (i, 0))
        out_shape = jax.ShapeDtypeStruct((n, D_MODEL), _f32)
    return pl.pallas_call(
        functools.partial(_final_kernel, n_tiles=nt),
        grid=(nt,),
        in_specs=[
            dspec(lambda i: (i,)),
            dspec(lambda i: (jnp.minimum(i + 1, nt - 1),)),
            pl.BlockSpec((ROW_TM, D_MODEL), lambda i: (i, 0)),
            pl.BlockSpec((ROW_TM, LANES), lambda i: (i, 0)),
            pl.BlockSpec((1, D_MODEL), lambda i: (0, 0)),
            pl.BlockSpec(memory_space=pl.ANY),
        ],
        out_specs=out_spec,
        out_shape=out_shape,
        scratch_shapes=[pltpu.VMEM((2, 2, ROW_TM * ROW_TILES, LANES), _f32), pltpu.SemaphoreType.DMA((2,))],
        compiler_params=_cparams(("arbitrary",)),
        name="moe_combine_final",
    )(dest_flat, dest_flat, x1, rw, g, y_disp)


def _moe_plan(route_e, rank, counts):
    counts = counts[0, :N_EXPERTS].astype(jnp.int32)
    padded = (counts + MOE_TM - 1) // MOE_TM * MOE_TM
    pend = jnp.cumsum(padded)
    pstart = pend - padded
    e = route_e[:, :2]
    dest = (pstart[e] + rank[:, :2]).reshape(-1)
    n_pairs = dest.shape[0]
    n_tiles = (n_pairs + N_EXPERTS * (MOE_TM - 1) + MOE_TM - 1) // MOE_TM
    tile_start = jnp.arange(n_tiles, dtype=jnp.int32) * MOE_TM
    tile_expert = jnp.minimum(jnp.sum(pend[None, :] <= tile_start[:, None], axis=1), N_EXPERTS - 1).astype(jnp.int32)
    n_used = (pend[-1] // MOE_TM).astype(jnp.int32).reshape(1)
    return dest, tile_expert, n_used, n_tiles


def kernel(x_prompt, x_sample, cache_a_k, cache_a_v, cache_b_k, cache_b_v, g_attn, w_in, attn_sinks, g_out_a,
           g_out_b, w_out, g_ffn, w_router_group, b_router_group, w_router_expert, b_router_expert, w_gate, w_up,
           w_down, g_final):
    depth = g_attn.shape[0]
    assert depth == 1
    b, t, _ = x_prompt.shape
    bd, s_len, _ = x_sample.shape
    assert t % (RES * BLOCK) == 0 and t % PROJ_TM == 0 and 8 % s_len == 0
    n_p, n_s = b * t, bd * s_len
    n_tok = n_p + n_s
    l = 0

    w_l = w_in[l]
    w_in_b = jnp.concatenate([w_l[:, :WIDTH_A], w_l[:, WIDTH_A + 2 * KV_WIDTH_A:],
                              w_l[:, WIDTH_A:WIDTH_A + 2 * KV_WIDTH_A]], axis=1).astype(_bf16)
    w_out_b = w_out[l].astype(_bf16)
    wg_b, wu_b, wd_b = w_gate[l].astype(_bf16), w_up[l].astype(_bf16), w_down[l].astype(_bf16)
    pad = LANES - N_GROUPS - N_EXPERTS
    w_r = jnp.concatenate([w_router_group[l], w_router_expert[l], jnp.zeros((D_MODEL, pad), _f32)], axis=1)
    b_r = jnp.concatenate([b_router_group[l], b_router_expert[l], jnp.zeros((pad,), _f32)])[None]
    g_a, g_oa, g_ob, g_f = g_attn[l][None], g_out_a[l][None], g_out_b[l][None], g_ffn[l][None]
    sinks = attn_sinks[l]

    cos_p, sin_p = _rope_tables(jnp.arange(t))
    p_nat, kv_f32, p_res = _project_prompt(x_prompt, g_a, w_in_b, cos_p, sin_p)
    (oa,) = _band_attention(p_nat[:, None], 1, t, (BLOCK,), (0, COL_KA // LANES, COL_VA // LANES), il=1, sinks=sinks)
    cb = (COL_QB // WIDTH_B, COL_KB // WIDTH_B, COL_VB // WIDTH_B)
    o1, l1 = _band_attention(p_nat[:, None], 1, t, (BLOCK,), cb, il=1)
    o4, l4 = _band_attention(p_res.reshape(b, 4, 4, t // RES, 3 * WIDTH_B), 4, t // 4, (4, BLOCK // 4), (0, 1, 2), il=4)
    o16, l16 = _band_attention(p_res, RES, t // RES, (BLOCK,), (0, 1, 2), il=1)

    upt = ROW_TM // RES
    tpb = t // ROW_TM
    sub4 = BLOCK // 4 // upt
    nat = lambda a, w: (a.reshape(n_p // RES, RES, w), pl.BlockSpec((upt, RES, w), lambda i: (i, 0, 0)))
    res4 = lambda a: (a.reshape(b, 4, t // RES // (BLOCK // 4), 4, BLOCK // 4, WIDTH_B),
                      pl.BlockSpec((None, 4, None, 4, upt, WIDTH_B),
                                   lambda i: (i // tpb, 0, (i % tpb) // sub4, 0, (i % tpb) % sub4, 0)))
    res16 = lambda a: (a, pl.BlockSpec((None, RES, upt, WIDTH_B), lambda i: (i // tpb, 0, i % tpb, 0)))
    obs_p = [nat(o1, WIDTH_B), res4(o4), res16(o16)]
    lses_p = [nat(l1, WIDTH_B), res4(l4), res16(l16)]

    cos_s, sin_s = _rope_tables(PAST_LEN + jnp.arange(n_s) % s_len)
    ps = _project_sample(x_sample.reshape(n_s, D_MODEL), g_a, w_in_b, cos_s, sin_s)
    to_t = lambda c: jnp.transpose(c[l], (0, 2, 3, 1))
    from_t = lambda c: jnp.transpose(c, (0, 3, 1, 2))[None]
    oa_s, ak_s, av_s = _sample_attention_a(ps, to_t(cache_a_k), to_t(cache_a_v), sinks, s_len)
    ob_s, bk_s, bv_s = _sample_attention_b(ps, to_t(cache_b_k), to_t(cache_b_v), s_len)

    consts = (g_oa, g_ob, w_out_b, g_f, w_r.astype(_bf16), b_r)
    x1_p, h2_p, re_p, rw_p = _merge(n_p, nat(oa, WIDTH_A), obs_p, lses_p, nat(x_prompt, D_MODEL),
                                    (False, True, True), *consts)
    plain = lambda a: (a, pl.BlockSpec((ROW_TM, a.shape[1]), lambda i: (i, 0)))
    x1_s, h2_s, re_s, rw_s = _merge(n_s, plain(oa_s), [plain(ob_s)], [], plain(x_sample.reshape(n_s, D_MODEL)),
                                    (False,), *consts)

    route_e = jnp.concatenate([re_p, re_s], axis=0)
    rank, counts = _rank(route_e)
    dest, tile_expert, n_used, n_tiles = _moe_plan(route_e, rank, counts)
    dest_p, dest_s = dest[:2 * n_p], dest[2 * n_p:]
    x_disp = jnp.zeros((n_tiles * MOE_TM * PACK_TILES, LANES), jnp.uint32)
    x_disp = _dispatch(dest_p, h2_p, x_disp)
    x_disp = _dispatch(dest_s, h2_s, x_disp)
    y_disp = _expert_ffn(tile_expert, n_used, x_disp, wg_b, wu_b, wd_b)
    g_fin = g_final[None]
    y_prompt = _final(dest_p, x1_p, rw_p, g_fin, y_disp, True).reshape(b, t, D_MODEL)
    y_sample = _final(dest_s, x1_s, rw_s, g_fin, y_disp, False).reshape(bd, s_len, D_MODEL)

    kv = kv_f32
    heads = lambda a, h: a.reshape(b, -1, h, HEAD_DIM)[None]
    bk_p = heads(kv[:, t - WINDOW_B:, 0:WIDTH_B], N_HEADS_B)
    bv_p = heads(kv[:, t - WINDOW_B:, WIDTH_B:2 * WIDTH_B], N_HEADS_B)
    ak_p = heads(kv[:, t - WINDOW_A:, 2 * WIDTH_B:2 * WIDTH_B + KV_WIDTH_A], N_KV_A)
    av_p = heads(kv[:, t - WINDOW_A:, 2 * WIDTH_B + KV_WIDTH_A:], N_KV_A)
    return (y_prompt, y_sample, ak_p, av_p, bk_p, bv_p, from_t(ak_s), from_t(av_s), from_t(bk_s), from_t(bv_s))
```
